```python
import math
import jax
import jax.numpy as jnp
from jax import lax
import numpy as np

D_MODEL = 4096
BATCH = 4
SEQ = 2048
DEPTH = 1
DEC_BATCH = 32
DEC_SEQ = 1
PAST_LEN = 8192
PAGE_SIZE = 128

H_A = 8
DH_A = 128
A_QK = H_A * 2 * DH_A
A_V = H_A * 2 * DH_A
Q_BLOCK = 128
H_G = 16
DK_G = 128
DV_G = 128
G_K = H_G * DK_G
G_V = H_G * DV_G
CONV_W = 4
C_CONV = 2 * G_K + G_V
GDN_CHUNK = 64
N_KEYS = 128
N_EXPERTS = N_KEYS * N_KEYS
P_HEADS = 8
P_DQ = 256
P_TOPK = 16
P_BLOCK = 128
IN_SIZES = (A_QK, A_QK, A_V, G_K, G_K, G_V, G_V, H_G, H_G, D_MODEL, D_MODEL)
N_IN = 2 * A_QK + A_V + 2 * G_K + 2 * G_V + 2 * H_G + 2 * D_MODEL
EPS = 1e-6

kernel_name = 'hybrid_diffattn_gdn_peer_step'


def _rmsnorm(x, g):
    xf = x.astype(jnp.float32)
    y = xf * lax.rsqrt(jnp.mean(xf * xf, axis=-1, keepdims=True) + EPS)
    return (y * g.astype(jnp.float32)).astype(x.dtype)


def _l2norm(x):
    xf = x.astype(jnp.float32)
    return xf * lax.rsqrt(jnp.sum(xf * xf, axis=-1, keepdims=True) + EPS)


def _diff_attention(q, q_pos, k_past, v_past, k_new, v_new, lam):
    B, T = q.shape[:2]
    P = k_past.shape[1]
    bs = min(Q_BLOCK, T)
    nb = -(-T // bs)
    pad = nb * bs - T
    qb = jnp.pad(q, ((0, 0), (0, pad), (0, 0), (0, 0))).reshape(B, nb, bs, H_A, 2, DH_A)
    qb = jnp.moveaxis(qb, 1, 0)
    pb = jnp.pad(q_pos, (0, pad), mode='edge').reshape(nb, bs)
    kp = k_past.reshape(B, P, H_A, 2, DH_A)
    kn = k_new.reshape(B, T, H_A, 2, DH_A)
    new_pos = P + jnp.arange(T, dtype=jnp.int32)
    scale = DH_A ** -0.5

    def block(args):
        qi, pi = args
        s = jnp.concatenate([
            jnp.einsum('bqhcd,bkhcd->bhcqk', qi, kp),
            jnp.einsum('bqhcd,bkhcd->bhcqk', qi, kn)], axis=-1).astype(jnp.float32) * scale
        mask = jnp.concatenate([jnp.ones((bs, P), dtype=bool), new_pos[None, :] <= pi[:, None]], axis=-1)
        pr = jax.nn.softmax(jnp.where(mask, s, -jnp.inf), axis=-1)
        p = (pr[:, :, 0] - lam * pr[:, :, 1]).astype(v_new.dtype)
        return (jnp.einsum('bhqk,bkhd->bqhd', p[..., :P], v_past)
                + jnp.einsum('bhqk,bkhd->bqhd', p[..., P:], v_new))

    o = lax.map(block, (qb, pb))
    return jnp.moveaxis(o, 0, 1).reshape(B, nb * bs, H_A, 2 * DH_A)[:, :T]


def _causal_conv(x, prev, w):
    T = x.shape[1]
    xp = jnp.concatenate([prev.astype(x.dtype), x], axis=1)
    y = xp[:, 0:T] * w[0]
    for i in range(1, CONV_W):
        y = y + xp[:, i:i + T] * w[i]
    return jax.nn.silu(y), xp[:, T:]


def _gated_delta_chunked(q, k, v, g, beta, s0):
    B, T = q.shape[:2]
    C = min(GDN_CHUNK, T)
    n = -(-T // C)
    pad = n * C - T

    def chunks(a):
        a = jnp.pad(a, [(0, 0), (0, pad)] + [(0, 0)] * (a.ndim - 2))
        a = a.reshape(B, n, C, *a.shape[2:])
        return jnp.moveaxis(jnp.moveaxis(a, 1, 0), 2, 3)

    qc, kc, vc, gch, bc = chunks(q), chunks(k), chunks(v), chunks(g), chunks(beta)
    gc = jnp.cumsum(gch, axis=-1)
    incl = jnp.tril(jnp.ones((C, C), dtype=bool))
    strict = jnp.tril(jnp.ones((C, C), dtype=bool), -1)
    decay = jnp.exp(jnp.where(incl, gc[..., :, None] - gc[..., None, :], -jnp.inf))
    kb = kc * bc[..., None]
    vb = vc * bc[..., None]
    low = jnp.where(strict, jnp.einsum('nbhid,nbhjd->nbhij', kb, kc) * decay, 0.0)
    sys_m = low + jnp.eye(C, dtype=jnp.float32)
    u = lax.linalg.triangular_solve(sys_m, vb, left_side=True, lower=True)
    w = lax.linalg.triangular_solve(sys_m, kb * jnp.exp(gc)[..., None], left_side=True, lower=True)
    attn = jnp.einsum('nbhid,nbhjd->nbhij', qc, kc) * decay
    qd = qc * jnp.exp(gc)[..., None]
    kd = kc * jnp.exp(gc[..., -1:] - gc)[..., None]
    glast = jnp.exp(gc[..., -1])

    def step(S, inp):
        u_c, w_c, a_c, qd_c, kd_c, gl_c = inp
        v_corr = u_c - jnp.einsum('bhcd,bhde->bhce', w_c, S)
        o = jnp.einsum('bhcd,bhde->bhce', qd_c, S) + jnp.einsum('bhij,bhje->bhie', a_c, v_corr)
        S = S * gl_c[..., None, None] + jnp.einsum('bhcd,bhce->bhde', kd_c, v_corr)
        return S, o

    S, o = lax.scan(step, s0, (u, w, attn, qd, kd, glast))
    o = o.transpose(1, 0, 3, 2, 4).reshape(B, n * C, H_G, DV_G)[:, :T]
    return o, S


def _peer(h, w_pq, sub_keys, peer_u, peer_v):
    n = h.shape[0]
    bs = min(P_BLOCK, n)
    nb = -(-n // bs)
    hp = jnp.pad(h, ((0, nb * bs - n), (0, 0))).reshape(nb, bs, D_MODEL)
    keys = sub_keys.astype(jnp.float32)

    def block(xb):
        q = (xb @ w_pq).astype(jnp.float32).reshape(bs, P_HEADS, 2, P_DQ // 2)
        s = jnp.einsum('nhpd,hpkd->nhpk', q, keys)
        s1, i1 = lax.top_k(s[:, :, 0], P_TOPK)
        s2, i2 = lax.top_k(s[:, :, 1], P_TOPK)
        cand = (s1[..., :, None] + s2[..., None, :]).reshape(bs, P_HEADS, P_TOPK * P_TOPK)
        cidx = (i1[..., :, None] * N_KEYS + i2[..., None, :]).reshape(bs, P_HEADS, P_TOPK * P_TOPK)
        top_s, pos = lax.top_k(cand, P_TOPK)
        eidx = jnp.take_along_axis(cidx, pos, axis=-1)
        gate = jax.nn.softmax(top_s, axis=-1)
        act = jax.nn.gelu(jnp.einsum('nd,nhkd->nhk', xb, peer_u[eidx]).astype(jnp.float32), approximate=False)
        return jnp.einsum('nhk,nhkd->nd', (gate * act).astype(xb.dtype), peer_v[eidx])

    return lax.map(block, hp).reshape(nb * bs, D_MODEL)[:n]


def _layer(x, c, k_past, v_past, conv_prev, gdn_prev, lam_init, w_ada, b_ada, g_pre_mix, g_post_mix,
           g_pre_ffn, g_post_ffn, w_in, lam_q1, lam_k1, lam_q2, lam_k2, g_attn_head, conv_w, a_log,
           dt_bias, g_gdn_head, w_up_a, w_up_g, w_o, w_pq, sub_keys, peer_u, peer_v):
    B, T, _ = x.shape
    P = k_past.shape[1]
    mod = jax.nn.silu(c) @ w_ada + b_ada
    sh1, sc1, ga1, sh2, sc2, ga2 = jnp.split(mod[:, None, :], 6, axis=-1)
    h = _rmsnorm(x, g_pre_mix) * (1 + sc1) + sh1
    proj = h @ w_in
    offsets = [int(o) for o in np.cumsum(IN_SIZES)[:-1]]
    aq, ak, av, gq, gk, gv, gz, ga, gb, gate_a, gate_g = jnp.split(proj, offsets, axis=-1)

    q_a = aq.reshape(B, T, H_A, 2 * DH_A)
    k_new = ak.reshape(B, T, H_A, 2 * DH_A)
    v_new = av.reshape(B, T, H_A, 2 * DH_A)
    lam = (jnp.exp(jnp.sum(lam_q1.astype(jnp.float32) * lam_k1.astype(jnp.float32)))
           - jnp.exp(jnp.sum(lam_q2.astype(jnp.float32) * lam_k2.astype(jnp.float32))) + lam_init)
    q_pos = P + jnp.arange(T, dtype=jnp.int32)
    o_a = _diff_attention(q_a, q_pos, k_past, v_past, k_new, v_new, lam)
    o_a = (_rmsnorm(o_a, g_attn_head) * (1 - lam_init)).reshape(B, T, A_V)

    conv_out, conv_new = _causal_conv(jnp.concatenate([gq, gk, gv], axis=-1), conv_prev, conv_w)
    cq, ck, cv = jnp.split(conv_out, [G_K, 2 * G_K], axis=-1)
    q_g = _l2norm(cq.reshape(B, T, H_G, DK_G)) * (DK_G ** -0.5)
    k_g = _l2norm(ck.reshape(B, T, H_G, DK_G))
    v_g = cv.reshape(B, T, H_G, DV_G).astype(jnp.float32)
    beta = jax.nn.sigmoid(gb.astype(jnp.float32))
    g_log = -jnp.exp(a_log.astype(jnp.float32)) * jax.nn.softplus(ga.astype(jnp.float32) + dt_bias.astype(jnp.float32))
    o_g, gdn_new = _gated_delta_chunked(q_g, k_g, v_g, g_log, beta, gdn_prev.astype(jnp.float32))
    o_g = (_rmsnorm(o_g.astype(x.dtype), g_gdn_head) * jax.nn.silu(gz.reshape(B, T, H_G, DV_G))).reshape(B, T, G_V)

    merged = jax.nn.sigmoid(gate_a) * (o_a @ w_up_a) + jax.nn.sigmoid(gate_g) * (o_g @ w_up_g)
    x = x + ga1 * _rmsnorm(merged @ w_o, g_post_mix)

    h2 = _rmsnorm(x, g_pre_ffn) * (1 + sc2) + sh2
    f = _peer(h2.reshape(B * T, D_MODEL), w_pq, sub_keys, peer_u, peer_v).reshape(B, T, D_MODEL)
    x = x + ga2 * _rmsnorm(f, g_post_ffn)
    return x, k_new, v_new, conv_new, gdn_new.astype(gdn_prev.dtype)


def setup_inputs(seed: int = 0) -> dict:
    key = jax.random.key(seed)
    ks = jax.random.split(key, 40)
    f32 = jnp.float32
    n_pages = PAST_LEN // PAGE_SIZE
    n_used = DEC_BATCH * n_pages
    n_pool = n_used + max(1, n_used // 4)

    def nrm(k, shape, s):
        return jax.random.normal(k, shape, f32) * s

    page_table = jax.random.permutation(ks[0], n_pool)[:n_used].astype(jnp.int32).reshape(DEC_BATCH, n_pages)
    dt = jnp.exp(jax.random.uniform(ks[1], (DEPTH, H_G), f32, math.log(1e-3), math.log(1e-1)))
    return {
        'x_prompt': nrm(ks[2], (BATCH, SEQ, D_MODEL), 1.0),
        'x_sample': nrm(ks[3], (DEC_BATCH, DEC_SEQ, D_MODEL), 1.0),
        'c_prompt': nrm(ks[4], (BATCH, D_MODEL), 1.0),
        'c_sample': nrm(ks[5], (DEC_BATCH, D_MODEL), 1.0),
        'cache_k': nrm(ks[6], (DEPTH, n_pool, PAGE_SIZE, H_A, 2 * DH_A), 1.0),
        'cache_v': nrm(ks[7], (DEPTH, n_pool, PAGE_SIZE, H_A, 2 * DH_A), 1.0),
        'page_table': page_table,
        'state_conv': nrm(ks[8], (DEPTH, DEC_BATCH, CONV_W - 1, C_CONV), 1.0),
        'state_gdn': nrm(ks[9], (DEPTH, DEC_BATCH, H_G, DK_G, DV_G), DK_G ** -0.5),
        'w_ada': nrm(ks[10], (DEPTH, D_MODEL, 6 * D_MODEL), 0.5 * D_MODEL ** -0.5),
        'b_ada': nrm(ks[11], (DEPTH, 6 * D_MODEL), 0.02),
        'g_pre_mix': 1.0 + nrm(ks[12], (DEPTH, D_MODEL), 0.1),
        'g_post_mix': 1.0 + nrm(ks[13], (DEPTH, D_MODEL), 0.1),
        'g_pre_ffn': 1.0 + nrm(ks[14], (DEPTH, D_MODEL), 0.1),
        'g_post_ffn': 1.0 + nrm(ks[15], (DEPTH, D_MODEL), 0.1),
        'w_in': nrm(ks[16], (DEPTH, D_MODEL, N_IN), D_MODEL ** -0.5),
        'lam_q1': nrm(ks[17], (DEPTH, DH_A), 0.1),
        'lam_k1': nrm(ks[18], (DEPTH, DH_A), 0.1),
        'lam_q2': nrm(ks[19], (DEPTH, DH_A), 0.1),
        'lam_k2': nrm(ks[20], (DEPTH, DH_A), 0.1),
        'g_attn_head': 1.0 + nrm(ks[21], (DEPTH, 2 * DH_A), 0.1),
        'conv_w': nrm(ks[22], (DEPTH, CONV_W, C_CONV), CONV_W ** -0.5),
        'a_log': jnp.log(jax.random.uniform(ks[23], (DEPTH, H_G), f32, 1.0, 16.0)),
        'dt_bias': jnp.log(jnp.expm1(dt)),
        'g_gdn_head': 1.0 + nrm(ks[24], (DEPTH, DV_G), 0.1),
        'w_up_a': nrm(ks[25], (DEPTH, A_V, D_MODEL), A_V ** -0.5),
        'w_up_g': nrm(ks[26], (DEPTH, G_V, D_MODEL), G_V ** -0.5),
        'w_o': nrm(ks[27], (DEPTH, D_MODEL, D_MODEL), D_MODEL ** -0.5),
        'w_pq': nrm(ks[28], (DEPTH, D_MODEL, P_HEADS * P_DQ), D_MODEL ** -0.5),
        'sub_keys': nrm(ks[29], (DEPTH, P_HEADS, 2, N_KEYS, P_DQ // 2), (P_DQ // 2) ** -0.5),
        'peer_u': nrm(ks[30], (DEPTH, N_EXPERTS, D_MODEL), D_MODEL ** -0.5),
        'peer_v': nrm(ks[31], (DEPTH, N_EXPERTS, D_MODEL), 1.0),
    }


def reference(x_prompt, x_sample, c_prompt, c_sample, cache_k, cache_v, page_table, state_conv, state_gdn,
              w_ada, b_ada, g_pre_mix, g_post_mix, g_pre_ffn, g_post_ffn, w_in, lam_q1, lam_k1, lam_q2,
              lam_k2, g_attn_head, conv_w, a_log, dt_bias, g_gdn_head, w_up_a, w_up_g, w_o, w_pq, sub_keys,
              peer_u, peer_v):
    n_seq, n_pages = page_table.shape
    past_len = n_pages * PAGE_SIZE
    bp = x_prompt.shape[0]
    yp, ys = x_prompt, x_sample
    kp_l, vp_l, cp_l, sp_l, ks_l, vs_l, cs_l, ss_l = [], [], [], [], [], [], [], []
    for l in range(DEPTH):
        lam_init = 0.8 - 0.6 * math.exp(-0.3 * l)
        wts = (w_ada[l], b_ada[l], g_pre_mix[l], g_post_mix[l], g_pre_ffn[l], g_post_ffn[l], w_in[l],
               lam_q1[l], lam_k1[l], lam_q2[l], lam_k2[l], g_attn_head[l], conv_w[l], a_log[l], dt_bias[l],
               g_gdn_head[l], w_up_a[l], w_up_g[l], w_o[l], w_pq[l], sub_keys[l], peer_u[l], peer_v[l])
        empty_kv = jnp.zeros((bp, 0, H_A, 2 * DH_A), x_prompt.dtype)
        conv0 = jnp.zeros((bp, CONV_W - 1, C_CONV), x_prompt.dtype)
        s0 = jnp.zeros((bp, H_G, DK_G, DV_G), state_gdn.dtype)
        yp, kp, vp, cp, sp = _layer(yp, c_prompt, empty_kv, empty_kv, conv0, s0, lam_init, *wts)
        k_past = cache_k[l, page_table].reshape(n_seq, past_len, H_A, 2 * DH_A)
        v_past = cache_v[l, page_table].reshape(n_seq, past_len, H_A, 2 * DH_A)
        ys, kn, vn, cn, sn = _layer(ys, c_sample, k_past, v_past, state_conv[l], state_gdn[l], lam_init, *wts)
        kp_l.append(kp); vp_l.append(vp); cp_l.append(cp); sp_l.append(sp)
        ks_l.append(kn); vs_l.append(vn); cs_l.append(cn); ss_l.append(sn)
    return (yp, ys, jnp.stack(kp_l), jnp.stack(vp_l), jnp.stack(cp_l), jnp.stack(sp_l),
            jnp.stack(ks_l), jnp.stack(vs_l), jnp.stack(cs_l), jnp.stack(ss_l))
```

```python
import functools
import math

import jax
import jax.numpy as jnp
from jax import lax
from jax.experimental import pallas as pl
from jax.experimental.pallas import tpu as pltpu

F32 = jnp.float32
BF16 = jnp.bfloat16

D_MODEL = 4096
H_A = 8
DH_A = 128
HD_A = 2 * DH_A
A_W = H_A * HD_A
H_G = 16
DK_G = 128
DV_G = 128
G_W = H_G * DK_G
CONV_W = 4
C_CONV = 3 * G_W
GDN_CHUNK = 128
N_KEYS = 128
N_EXPERTS = N_KEYS * N_KEYS
P_HEADS = 8
P_TOPK = 16
PAGE = 128
EPS = 1e-6
LANES = 128
MIB = 2 ** 20

NT_DIMS = (((1,), (1,)), ((), ()))
TN_DIMS = (((0,), (0,)), ((), ()))


def _cp(semantics, vmem_mib=48):
    return pltpu.CompilerParams(dimension_semantics=semantics, vmem_limit_bytes=vmem_mib * MIB)


def _dot(a, b, dims=None, precision=None):
    if dims is None:
        return jnp.dot(a, b, preferred_element_type=F32, precision=precision)
    return lax.dot_general(a, b, dims, preferred_element_type=F32, precision=precision)


def _silu(x):
    return x * jax.nn.sigmoid(x)


def _rms(x, g):
    return x * lax.rsqrt(jnp.mean(x * x, axis=-1, keepdims=True) + EPS) * g


def _mod_body(c_ref, w_ref, b_ref, o_ref):
    sc = _silu(c_ref[...]).astype(BF16)
    o_ref[...] = _dot(sc, w_ref[...].astype(BF16)) + b_ref[...]


def ada_mod(c_all, w_ada, b_ada):
    r, d = c_all.shape
    n = w_ada.shape[1]
    tn = 512
    return pl.pallas_call(
        _mod_body,
        grid=(n // tn,),
        in_specs=[pl.BlockSpec((r, d), lambda j: (0, 0)),
                  pl.BlockSpec((d, tn), lambda j: (0, j)),
                  pl.BlockSpec((1, tn), lambda j: (0, j))],
        out_specs=pl.BlockSpec((r, tn), lambda j: (0, j)),
        out_shape=jax.ShapeDtypeStruct((r, n), F32),
        compiler_params=_cp(("parallel",)),
        name="ada_mod",
    )(c_all, w_ada, b_ada.reshape(1, n))


def _mod_spec(per_token, tm, seq, chunk):
    if per_token:
        return pl.BlockSpec((tm, D_MODEL), lambda i: (i, chunk))
    return pl.BlockSpec((None, None, 1, D_MODEL), lambda i: ((i * tm) // seq, chunk, 0, 0))


def _mod_arg(mod, per_token):
    return mod if per_token else mod.reshape(mod.shape[0], 6, 1, D_MODEL)


def _prenorm_body(x_ref, g_ref, sc_ref, sh_ref, o_ref):
    y = _rms(x_ref[...], g_ref[...])
    o_ref[...] = (y * (1 + sc_ref[...]) + sh_ref[...]).astype(o_ref.dtype)


def prenorm(x, g, mod, per_token, seq, tm):
    m = x.shape[0]
    ma = _mod_arg(mod, per_token)
    return pl.pallas_call(
        _prenorm_body,
        grid=(m // tm,),
        in_specs=[pl.BlockSpec((tm, D_MODEL), lambda i: (i, 0)),
                  pl.BlockSpec((1, D_MODEL), lambda i: (0, 0)),
                  _mod_spec(per_token, tm, seq, 1),
                  _mod_spec(per_token, tm, seq, 0)],
        out_specs=pl.BlockSpec((tm, D_MODEL), lambda i: (i, 0)),
        out_shape=jax.ShapeDtypeStruct((m, D_MODEL), BF16),
        compiler_params=_cp(("parallel",)),
        name="prenorm",
    )(x, g.reshape(1, D_MODEL), ma, ma)


def _mm_body(x_ref, w_ref, o_ref):
    o_ref[...] = _dot(x_ref[...], w_ref[...]).astype(o_ref.dtype)


def matmul(x, w, out_dtype, tm, tn):
    m, k = x.shape
    n = w.shape[1]
    tm, tn = min(tm, m), min(tn, n)
    return pl.pallas_call(
        _mm_body,
        grid=(m // tm, n // tn),
        in_specs=[pl.BlockSpec((tm, k), lambda i, j: (i, 0)),
                  pl.BlockSpec((k, tn), lambda i, j: (0, j))],
        out_specs=pl.BlockSpec((tm, tn), lambda i, j: (i, j)),
        out_shape=jax.ShapeDtypeStruct((m, n), out_dtype),
        compiler_params=_cp(("parallel", "parallel")),
        name="matmul",
    )(x, w)


def _lam_value(lam_ref, lam_init):
    l = lam_ref[...]
    a = jnp.sum(l[0:1] * l[1:2], axis=-1, keepdims=True)
    b = jnp.sum(l[2:3] * l[3:4], axis=-1, keepdims=True)
    return jnp.exp(a) - jnp.exp(b) + lam_init


def _attn_body(lam_ref, q_ref, k_ref, v_ref, g_ref, o_ref, m_sc, l_sc, acc_sc, *, lam_init):
    qi = pl.program_id(2)
    ki = pl.program_id(3)
    tq, tk = q_ref.shape[0], k_ref.shape[0]
    scale = DH_A ** -0.5

    @pl.when(ki == 0)
    def _():
        m_sc[...] = jnp.full(m_sc.shape, -jnp.inf, F32)
        l_sc[...] = jnp.zeros(l_sc.shape, F32)
        acc_sc[...] = jnp.zeros(acc_sc.shape, F32)

    @pl.when(ki <= qi)
    def _():
        q = q_ref[...]
        k = k_ref[...].astype(BF16)
        v = v_ref[...].astype(BF16)
        row = lax.broadcasted_iota(jnp.int32, (tq, tk), 0)
        col = lax.broadcasted_iota(jnp.int32, (tq, tk), 1)
        visible = (col <= row) | (ki < qi)
        for c in range(2):
            sl = slice(c * DH_A, (c + 1) * DH_A)
            s = _dot(q[:, sl], k[:, sl], NT_DIMS) * scale
            s = jnp.where(visible, s, -jnp.inf)
            m_prev = m_sc[c]
            m_new = jnp.maximum(m_prev, jnp.max(s, axis=1, keepdims=True))
            alpha = jnp.exp(m_prev - m_new)
            p = jnp.exp(s - m_new)
            l_sc[c] = alpha * l_sc[c] + jnp.sum(p, axis=1, keepdims=True)
            acc_sc[c] = alpha * acc_sc[c] + _dot(p.astype(BF16), v)
            m_sc[c] = m_new

    @pl.when(ki == qi)
    def _():
        lam = _lam_value(lam_ref, lam_init)
        o = acc_sc[0] / l_sc[0] - lam * (acc_sc[1] / l_sc[1])
        o_ref[...] = (_rms(o, g_ref[...]) * (1 - lam_init)).astype(o_ref.dtype)


def attn_prompt(q, k, v, lam_vecs, g_head, batch, seq, lam_init, tq=512):
    nq = seq // tq
    kernel = functools.partial(_attn_body, lam_init=lam_init)
    return pl.pallas_call(
        kernel,
        grid=(batch, H_A, nq, nq),
        in_specs=[pl.BlockSpec((4, DH_A), lambda b, h, i, j: (0, 0)),
                  pl.BlockSpec((tq, HD_A), lambda b, h, i, j: (b * nq + i, h)),
                  pl.BlockSpec((tq, HD_A), lambda b, h, i, j: (b * nq + jnp.minimum(i, j), h)),
                  pl.BlockSpec((tq, HD_A), lambda b, h, i, j: (b * nq + jnp.minimum(i, j), h)),
                  pl.BlockSpec((1, HD_A), lambda b, h, i, j: (0, 0))],
        out_specs=pl.BlockSpec((tq, HD_A), lambda b, h, i, j: (b * nq + i, h)),
        out_shape=jax.ShapeDtypeStruct((batch * seq, A_W), BF16),
        scratch_shapes=[pltpu.VMEM((2, tq, 1), F32), pltpu.VMEM((2, tq, 1), F32),
                        pltpu.VMEM((2, tq, HD_A), F32)],
        compiler_params=_cp(("parallel", "parallel", "parallel", "arbitrary")),
        name="attn_prompt",
    )(lam_vecs, q, k, v, g_head.reshape(1, HD_A))


def _attn_dec_body(pt_ref, lam_ref, q_ref, kn_ref, vn_ref, g_ref, kc_ref, vc_ref, o_ref,
                   m_sc, l_sc, acc_sc, *, lam_init):
    p = pl.program_id(1)
    n_pages = pl.num_programs(1)
    nr = 2 * H_A
    scale = DH_A ** -0.5
    rowi = lax.broadcasted_iota(jnp.int32, (nr, A_W), 0)
    lanei = lax.broadcasted_iota(jnp.int32, (nr, A_W), 1)
    qrows = jnp.where(lanei // DH_A == rowi, jnp.broadcast_to(q_ref[...], (nr, A_W)), 0.0)

    @pl.when(p == 0)
    def _():
        m_sc[...] = jnp.full(m_sc.shape, -jnp.inf, F32)
        l_sc[...] = jnp.zeros(l_sc.shape, F32)
        acc_sc[...] = jnp.zeros(acc_sc.shape, F32)

    s = _dot(qrows.astype(BF16), kc_ref[...].astype(BF16), NT_DIMS) * scale
    m_prev = m_sc[...]
    m_new = jnp.maximum(m_prev, jnp.max(s, axis=1, keepdims=True))
    alpha = jnp.exp(m_prev - m_new)
    pr = jnp.exp(s - m_new)
    l_sc[...] = alpha * l_sc[...] + jnp.sum(pr, axis=1, keepdims=True)
    acc_sc[...] = alpha * acc_sc[...] + _dot(pr.astype(BF16), vc_ref[...].astype(BF16))
    m_sc[...] = m_new

    @pl.when(p == n_pages - 1)
    def _():
        kn = kn_ref[...].astype(BF16).astype(F32)
        vn = vn_ref[...].astype(BF16).astype(F32)
        qb = qrows.astype(BF16).astype(F32)
        sn = jnp.sum(qb * kn, axis=1, keepdims=True) * scale
        m_prev = m_sc[...]
        m_new = jnp.maximum(m_prev, sn)
        alpha = jnp.exp(m_prev - m_new)
        pn = jnp.exp(sn - m_new)
        l_fin = alpha * l_sc[...] + pn
        acc = alpha * acc_sc[...] + pn.astype(BF16).astype(F32) * vn
        lam = _lam_value(lam_ref, lam_init)
        coef = jnp.where(rowi % 2 == 0, 1.0, -lam)
        own = lanei // HD_A == rowi // 2
        o_row = jnp.sum(jnp.where(own, acc / l_fin * coef, 0.0), axis=0, keepdims=True)
        hrow = lax.broadcasted_iota(jnp.int32, (H_A, A_W), 0)
        hlane = lax.broadcasted_iota(jnp.int32, (H_A, A_W), 1)
        o8 = jnp.where(hlane // HD_A == hrow, jnp.broadcast_to(o_row, (H_A, A_W)), 0.0)
        ms = jnp.sum(o8 * o8, axis=1, keepdims=True) * (1.0 / HD_A)
        o8 = o8 * lax.rsqrt(ms + EPS)
        o_ref[...] = jnp.sum(o8, axis=0, keepdims=True) * g_ref[...] * (1 - lam_init)


def attn_decode(q, k_new, v_new, cache_k, cache_v, page_table, lam_vecs, g_head, lam_init):
    nb, n_pages = page_table.shape
    pool = cache_k.shape[0]
    kc = cache_k.reshape(pool, PAGE, A_W)
    vc = cache_v.reshape(pool, PAGE, A_W)
    row3 = lambda a: a.reshape(nb, 1, A_W)
    kernel = functools.partial(_attn_dec_body, lam_init=lam_init)
    tok = pl.BlockSpec((None, 1, A_W), lambda b, p, pt: (b, 0, 0))
    page = pl.BlockSpec((None, PAGE, A_W), lambda b, p, pt: (pt[b * n_pages + p], 0, 0))
    out = pl.pallas_call(
        kernel,
        grid_spec=pltpu.PrefetchScalarGridSpec(
            num_scalar_prefetch=1,
            grid=(nb, n_pages),
            in_specs=[pl.BlockSpec((4, DH_A), lambda b, p, pt: (0, 0)), tok, tok, tok,
                      pl.BlockSpec((1, A_W), lambda b, p, pt: (0, 0)), page, page],
            out_specs=tok,
            scratch_shapes=[pltpu.VMEM((2 * H_A, 1), F32), pltpu.VMEM((2 * H_A, 1), F32),
                            pltpu.VMEM((2 * H_A, A_W), F32)]),
        out_shape=jax.ShapeDtypeStruct((nb, 1, A_W), F32),
        compiler_params=_cp(("parallel", "arbitrary")),
        name="attn_decode",
    )(page_table.reshape(-1), lam_vecs, row3(q), row3(k_new), row3(v_new),
      jnp.tile(g_head, H_A).reshape(1, A_W), kc, vc)
    return out.reshape(nb, A_W)


def _softplus(x):
    return jnp.maximum(x, 0.0) + jnp.log1p(jnp.exp(-jnp.abs(x)))


def _gate_columns(gab, al_ref, dtb_ref, h):
    lane = lax.broadcasted_iota(jnp.int32, gab.shape, 1)
    g_all = -jnp.exp(al_ref[...]) * _softplus(gab + dtb_ref[...])
    g = jnp.sum(jnp.where(lane == h, g_all, 0.0), axis=1, keepdims=True)
    beta = jnp.sum(jnp.where(lane == H_G + h, jax.nn.sigmoid(gab), 0.0), axis=1, keepdims=True)
    return g, beta


def _gdn_body(xq_ref, xk_ref, xv_ref, z_ref, wq_ref, wk_ref, wv_ref, gab_ref, al_ref, dtb_ref, gh_ref,
              o_ref, s_ref, q_sc, k_sc, v_sc, gc_sc, beta_sc, qe_sc, oc_sc, m_sc, b_sc, gl_sc, st_sc):
    h = pl.program_id(1)
    t = xq_ref.shape[0]
    c_len = GDN_CHUNK
    n_chunks = t // c_len
    row = lax.broadcasted_iota(jnp.int32, (t, DK_G), 0)

    def conv(x_ref, w_ref):
        x = x_ref[...]
        w = w_ref[...]
        y = jnp.where(row >= 3, pltpu.roll(x, 3, 0), 0.0) * w[0:1]
        y = y + jnp.where(row >= 2, pltpu.roll(x, 2, 0), 0.0) * w[1:2]
        y = y + jnp.where(row >= 1, pltpu.roll(x, 1, 0), 0.0) * w[2:3]
        y = y + x * w[3:4]
        return _silu(y)

    cq = conv(xq_ref, wq_ref)
    q_sc[...] = cq * lax.rsqrt(jnp.sum(cq * cq, axis=-1, keepdims=True) + EPS) * (DK_G ** -0.5)
    ck = conv(xk_ref, wk_ref)
    k_sc[...] = ck * lax.rsqrt(jnp.sum(ck * ck, axis=-1, keepdims=True) + EPS)
    v_sc[...] = conv(xv_ref, wv_ref)

    g, beta = _gate_columns(gab_ref[...], al_ref, dtb_ref, h)
    beta_sc[...] = jnp.broadcast_to(beta, (t, DK_G))
    gc = jnp.broadcast_to(g, (t, DK_G))
    rc = row % c_len
    shift = 1
    while shift < c_len:
        gc = gc + jnp.where(rc >= shift, pltpu.roll(gc, shift, 0), 0.0)
        shift *= 2
    gc_sc[...] = gc

    ii = lax.broadcasted_iota(jnp.int32, (c_len, c_len), 0)
    jj = lax.broadcasted_iota(jnp.int32, (c_len, c_len), 1)

    def chunk_ops(c, carry):
        rows = pl.ds(pl.multiple_of(c * c_len, c_len), c_len)
        qc, kc, vc = q_sc[rows, :], k_sc[rows, :], v_sc[rows, :]
        gcc, bc = gc_sc[rows, :], beta_sc[rows, :]
        kb, vb = kc * bc, vc * bc
        diff = gcc - gcc.T
        decay = jnp.exp(jnp.where(ii >= jj, diff, -jnp.inf))
        kcb = kc.astype(BF16)
        kk = _dot(kb.astype(BF16), kcb, NT_DIMS)
        qk = _dot(qc.astype(BF16), kcb, NT_DIMS)
        low = jnp.where(ii > jj, kk * decay, 0.0)
        y = -jnp.where((ii // 2 == jj // 2), low, 0.0)
        s = 2
        while s < c_len:
            off = jnp.where(ii // (2 * s) == jj // (2 * s), jnp.where(ii // s != jj // s, low, 0.0), 0.0)
            offb = off.astype(BF16)
            g = off + _dot(y.astype(BF16), offb)
            y = y - g - _dot(g.astype(BF16), y.astype(BF16))
            s *= 2
        eg = jnp.exp(gcc)
        kbg = kb * eg
        yb = y.astype(BF16)
        u = vb + _dot(yb, vb.astype(BF16))
        w = kbg + _dot(yb, kbg.astype(BF16))
        attn = jnp.where(ii >= jj, qk * decay, 0.0).astype(BF16)
        g_last = gcc[c_len - 1:c_len, :]
        kd = (kc * jnp.exp(g_last - gcc)).astype(BF16)
        ub, wb = u.astype(BF16), w.astype(BF16)
        qe_sc[rows, :] = (qc * eg - _dot(attn, wb)).astype(BF16)
        oc_sc[rows, :] = _dot(attn, ub)
        m_sc[c] = _dot(kd, wb, TN_DIMS).astype(BF16)
        b_sc[c] = _dot(kd, ub, TN_DIMS)
        gl_sc[c] = jnp.exp(g_last)
        return carry

    lax.fori_loop(0, n_chunks, chunk_ops, 0)

    def scan(c, s):
        sb = s.astype(BF16)
        st_sc[c] = sb
        return s * gl_sc[c] - _dot(m_sc[c], sb) + b_sc[c]

    s_ref[...] = lax.fori_loop(0, n_chunks, scan, jnp.zeros((DK_G, DV_G), F32))

    def emit(c, carry):
        rows = pl.ds(pl.multiple_of(c * c_len, c_len), c_len)
        o = _dot(qe_sc[rows, :], st_sc[c]) + oc_sc[rows, :]
        o_ref[rows, :] = (_rms(o, gh_ref[...]) * _silu(z_ref[rows, :])).astype(o_ref.dtype)
        return carry

    lax.fori_loop(0, n_chunks, emit, 0)


def gdn_prompt(gproj, gab, conv_w, a_log, dt_bias, g_head, batch, seq):
    hd = lambda off: pl.BlockSpec((seq, DK_G), lambda b, h: (b, off + h))
    cw = lambda off: pl.BlockSpec((CONV_W, DK_G), lambda b, h: (0, off + h))
    vec = pl.BlockSpec((1, LANES), lambda b, h: (0, 0))
    nc = seq // GDN_CHUNK
    pad = lambda a: jnp.pad(a, (0, LANES - a.shape[0])).reshape(1, LANES)
    o, s = pl.pallas_call(
        _gdn_body,
        grid=(batch, H_G),
        in_specs=[hd(0), hd(H_G), hd(2 * H_G), hd(3 * H_G), cw(0), cw(H_G), cw(2 * H_G),
                  pl.BlockSpec((seq, LANES), lambda b, h: (b, 0)), vec, vec, vec],
        out_specs=[pl.BlockSpec((seq, DV_G), lambda b, h: (b, h)),
                   pl.BlockSpec((None, None, DK_G, DV_G), lambda b, h: (b, h, 0, 0))],
        out_shape=[jax.ShapeDtypeStruct((batch * seq, G_W), BF16),
                   jax.ShapeDtypeStruct((batch, H_G, DK_G, DV_G), F32)],
        scratch_shapes=[pltpu.VMEM((seq, DK_G), F32), pltpu.VMEM((seq, DK_G), F32),
                        pltpu.VMEM((seq, DV_G), F32), pltpu.VMEM((seq, DK_G), F32),
                        pltpu.VMEM((seq, DK_G), F32), pltpu.VMEM((seq, DK_G), BF16),
                        pltpu.VMEM((seq, DV_G), F32), pltpu.VMEM((nc, DK_G, DK_G), BF16),
                        pltpu.VMEM((nc, DK_G, DV_G), F32), pltpu.VMEM((nc, 1, LANES), F32),
                        pltpu.VMEM((nc, DK_G, DV_G), BF16)],
        compiler_params=_cp(("parallel", "parallel")),
        name="gdn_prompt",
    )(gproj, gproj, gproj, gproj, conv_w, conv_w, conv_w, gab, pad(a_log), pad(dt_bias),
      g_head.reshape(1, DV_G))
    return o, s


def _gdn_dec_body(x_ref, prev_ref, w_ref, gab_ref, al_ref, dtb_ref, gh_ref, s0_ref, o_ref, s_ref):
    hi = lax.Precision.HIGHEST
    x = x_ref[...]
    prev = prev_ref[...]
    w = w_ref[...]
    y = prev[0:1] * w[0:1]
    y = y + prev[1:2] * w[1:2]
    y = y + prev[2:3] * w[2:3]
    y = _silu(y + x[:, :C_CONV] * w[3:4])
    gab = gab_ref[...]
    g_all = -jnp.exp(al_ref[...]) * _softplus(gab + dtb_ref[...])
    b_all = jax.nn.sigmoid(gab)
    r8 = lax.broadcasted_iota(jnp.int32, (8, DK_G), 0)
    for h in range(H_G):
        sl = lambda base: slice(base + h * DK_G, base + (h + 1) * DK_G)
        cq, ck, v = y[:, sl(0)], y[:, sl(G_W)], y[:, sl(2 * G_W)]
        qn = cq * lax.rsqrt(jnp.sum(cq * cq, axis=-1, keepdims=True) + EPS) * (DK_G ** -0.5)
        kn = ck * lax.rsqrt(jnp.sum(ck * ck, axis=-1, keepdims=True) + EPS)
        eg = jnp.exp(g_all[:, h:h + 1])
        beta = b_all[:, H_G + h:H_G + h + 1]
        s0 = s0_ref[h]
        lhs = jnp.where(r8 == 0, jnp.broadcast_to(kn, (8, DK_G)),
                        jnp.where(r8 == 1, jnp.broadcast_to(qn, (8, DK_G)), 0.0))
        proj = _dot(lhs, s0, precision=hi)
        v_corr = beta * v - (beta * eg) * proj[0:1]
        o = eg * proj[1:2] + jnp.sum(qn * kn, axis=-1, keepdims=True) * v_corr
        k8 = jnp.where(r8 == 0, jnp.broadcast_to(kn, (8, DK_G)), 0.0)
        v8 = jnp.broadcast_to(v_corr, (8, DV_G))
        s_ref[h] = s0 * eg + _dot(k8, v8, TN_DIMS, precision=hi)
        z = x[:, sl(3 * G_W)]
        o_ref[:, h * DV_G:(h + 1) * DV_G] = _rms(o, gh_ref[...]) * _silu(z)


def gdn_decode(gproj, gab, conv_prev, conv_w, a_log, dt_bias, g_head, s0):
    nb = gproj.shape[0]
    vec = pl.BlockSpec((1, LANES), lambda b: (0, 0))
    pad = lambda a: jnp.pad(a, (0, LANES - a.shape[0])).reshape(1, LANES)
    o, s = pl.pallas_call(
        _gdn_dec_body,
        grid=(nb,),
        in_specs=[pl.BlockSpec((None, 1, 4 * G_W), lambda b: (b, 0, 0)),
                  pl.BlockSpec((None, CONV_W - 1, C_CONV), lambda b: (b, 0, 0)),
                  pl.BlockSpec((CONV_W, C_CONV), lambda b: (0, 0)),
                  pl.BlockSpec((None, 1, LANES), lambda b: (b, 0, 0)), vec, vec, vec,
                  pl.BlockSpec((None, H_G, DK_G, DV_G), lambda b: (b, 0, 0, 0))],
        out_specs=[pl.BlockSpec((None, 1, G_W), lambda b: (b, 0, 0)),
                   pl.BlockSpec((None, H_G, DK_G, DV_G), lambda b: (b, 0, 0, 0))],
        out_shape=[jax.ShapeDtypeStruct((nb, 1, G_W), F32),
                   jax.ShapeDtypeStruct((nb, H_G, DK_G, DV_G), F32)],
        compiler_params=_cp(("parallel",)),
        name="gdn_decode",
    )(gproj.reshape(nb, 1, 4 * G_W), conv_prev, conv_w, gab.reshape(nb, 1, LANES),
      pad(a_log), pad(dt_bias), g_head.reshape(1, DV_G), s0)
    return o.reshape(nb, G_W), s


def _merge_body(oa_ref, og_ref, wa_ref, wg_ref, ga_ref, gg_ref, o_ref):
    a = jax.nn.sigmoid(ga_ref[...]) * _dot(oa_ref[...], wa_ref[...])
    g = jax.nn.sigmoid(gg_ref[...]) * _dot(og_ref[...], wg_ref[...])
    o_ref[...] = (a + g).astype(o_ref.dtype)


def merge(o_a, o_g, w_up_a, w_up_g, gates, tm, tn=512):
    m = o_a.shape[0]
    tm = min(tm, m)
    nj = D_MODEL // tn
    return pl.pallas_call(
        _merge_body,
        grid=(m // tm, nj),
        in_specs=[pl.BlockSpec((tm, A_W), lambda i, j: (i, 0)),
                  pl.BlockSpec((tm, G_W), lambda i, j: (i, 0)),
                  pl.BlockSpec((A_W, tn), lambda i, j: (0, j)),
                  pl.BlockSpec((G_W, tn), lambda i, j: (0, j)),
                  pl.BlockSpec((tm, tn), lambda i, j: (i, j)),
                  pl.BlockSpec((tm, tn), lambda i, j: (i, nj + j))],
        out_specs=pl.BlockSpec((tm, tn), lambda i, j: (i, j)),
        out_shape=jax.ShapeDtypeStruct((m, D_MODEL), BF16),
        compiler_params=_cp(("parallel", "parallel")),
        name="merge",
    )(o_a, o_g, w_up_a, w_up_g, gates, gates)


def _postmix_body(x_ref, mix_ref, gpost_ref, gpre_ref, ga_ref, sc_ref, sh_ref, x1_ref, h2_ref):
    x1 = x_ref[...] + ga_ref[...] * _rms(mix_ref[...], gpost_ref[...])
    x1_ref[...] = x1
    h2_ref[...] = (_rms(x1, gpre_ref[...]) * (1 + sc_ref[...]) + sh_ref[...]).astype(h2_ref.dtype)


def postmix(x, mix, g_post, g_pre, mod, per_token, seq, tm):
    m = x.shape[0]
    ma = _mod_arg(mod, per_token)
    rowblk = pl.BlockSpec((tm, D_MODEL), lambda i: (i, 0))
    vec = pl.BlockSpec((1, D_MODEL), lambda i: (0, 0))
    return pl.pallas_call(
        _postmix_body,
        grid=(m // tm,),
        in_specs=[rowblk, rowblk, vec, vec, _mod_spec(per_token, tm, seq, 2),
                  _mod_spec(per_token, tm, seq, 4), _mod_spec(per_token, tm, seq, 3)],
        out_specs=[rowblk, rowblk],
        out_shape=[jax.ShapeDtypeStruct((m, D_MODEL), F32), jax.ShapeDtypeStruct((m, D_MODEL), BF16)],
        compiler_params=_cp(("parallel",)),
        name="postmix",
    )(x, mix, g_post.reshape(1, D_MODEL), g_pre.reshape(1, D_MODEL), ma, ma, ma)


def _final_body(x_ref, f_ref, g_ref, ga_ref, o_ref):
    o_ref[...] = x_ref[...] + ga_ref[...] * _rms(f_ref[...], g_ref[...])


def final_residual(x1, f, g_post, mod, per_token, seq, tm):
    m = x1.shape[0]
    ma = _mod_arg(mod, per_token)
    rowblk = pl.BlockSpec((tm, D_MODEL), lambda i: (i, 0))
    return pl.pallas_call(
        _final_body,
        grid=(m // tm,),
        in_specs=[rowblk, rowblk, pl.BlockSpec((1, D_MODEL), lambda i: (0, 0)),
                  _mod_spec(per_token, tm, seq, 5)],
        out_specs=rowblk,
        out_shape=jax.ShapeDtypeStruct((m, D_MODEL), F32),
        compiler_params=_cp(("parallel",)),
        name="final_residual",
    )(x1, f, g_post.reshape(1, D_MODEL), ma)


def _top_values(x, k):
    vals = []
    for _ in range(k):
        m = jnp.max(x, axis=0, keepdims=True)
        vals.append(m)
        x = jnp.where(x == m, -jnp.inf, x)
    return jnp.concatenate(vals, axis=0)


def _route_body(q_ref, keys_ref, st_ref, stats_ref):
    tb = q_ref.shape[0]
    for h in range(P_HEADS):
        tops = []
        for p in range(2):
            col = (2 * h + p) * N_KEYS
            s = _dot(keys_ref[h, p], q_ref[:, col:col + N_KEYS], NT_DIMS)
            st_ref[h, p] = s
            tops.append(_top_values(s, P_TOPK))
        v1, v2 = tops
        cand = jnp.concatenate([v1[a:a + 1] + v2 for a in range(P_TOPK)], axis=0)
        best = _top_values(cand, P_TOPK)
        z = jnp.sum(jnp.exp(best - best[0:1]), axis=0, keepdims=True)
        stats_ref[h] = jnp.concatenate(
            [best[P_TOPK - 1:P_TOPK], v1[0:1], v2[0:1], 1.0 / z, jnp.zeros((4, tb), F32)], axis=0)


def peer_route(qp, keys, tb):
    n = qp.shape[0]
    return pl.pallas_call(
        _route_body,
        grid=(n // tb,),
        in_specs=[pl.BlockSpec((tb, 2 * P_HEADS * N_KEYS), lambda i: (i, 0)),
                  pl.BlockSpec((P_HEADS, 2, N_KEYS, N_KEYS), lambda i: (0, 0, 0, 0))],
        out_specs=[pl.BlockSpec((P_HEADS, 2, N_KEYS, tb), lambda i: (0, 0, 0, i)),
                   pl.BlockSpec((P_HEADS, 8, tb), lambda i: (0, 0, i))],
        out_shape=[jax.ShapeDtypeStruct((P_HEADS, 2, N_KEYS, n), F32),
                   jax.ShapeDtypeStruct((P_HEADS, 8, n), F32)],
        compiler_params=_cp(("parallel",)),
        name="peer_route",
    )(qp, keys)


def _peer_body(h_ref, u_ref, v_ref, st_ref, stats_ref, o_ref):
    ei = pl.program_id(1)
    eb = u_ref.shape[0]
    rows_per_step = eb // N_KEYS

    @pl.when(ei == 0)
    def _():
        o_ref[...] = jnp.zeros(o_ref.shape, F32)

    a = _dot(u_ref[...], h_ref[...], NT_DIMS)
    act = 0.5 * a * (1.0 + lax.erf(a * (2.0 ** -0.5)))
    parts = []
    for r in range(rows_per_step):
        i = ei * rows_per_step + r
        wsum = None
        for h in range(P_HEADS):
            s1 = st_ref[h, 0, pl.ds(i, 1), :]
            c1 = jnp.exp(s1 - stats_ref[h, 1:2, :]) * stats_ref[h, 3:4, :]
            s2 = st_ref[h, 1]
            wgt = jnp.where(s1 + s2 >= stats_ref[h, 0:1, :], jnp.exp(s2 - stats_ref[h, 2:3, :]) * c1, 0.0)
            wsum = wgt if wsum is None else wsum + wgt
        parts.append((wsum * act[r * N_KEYS:(r + 1) * N_KEYS]).astype(BF16))
    wg = jnp.concatenate(parts, axis=0)
    o_ref[...] += _dot(wg, v_ref[...], TN_DIMS)


def peer_experts(h2, u, v, st, stats, tb, eb=512):
    n = h2.shape[0]
    return pl.pallas_call(
        _peer_body,
        grid=(n // tb, N_EXPERTS // eb),
        in_specs=[pl.BlockSpec((tb, D_MODEL), lambda i, e: (i, 0)),
                  pl.BlockSpec((eb, D_MODEL), lambda i, e: (e, 0)),
                  pl.BlockSpec((eb, D_MODEL), lambda i, e: (e, 0)),
                  pl.BlockSpec((P_HEADS, 2, N_KEYS, tb), lambda i, e: (0, 0, 0, i)),
                  pl.BlockSpec((P_HEADS, 8, tb), lambda i, e: (0, 0, i))],
        out_specs=pl.BlockSpec((tb, D_MODEL), lambda i, e: (i, 0)),
        out_shape=jax.ShapeDtypeStruct((n, D_MODEL), F32),
        compiler_params=_cp(("parallel", "arbitrary"), vmem_mib=56),
        name="peer_experts",
    )(h2, u, v, st, stats)


def peer(h2, w_pq, keys, u, v, tb):
    qp = matmul(h2, w_pq, BF16, 1024, 512)
    st, stats = peer_route(qp, keys, min(tb, 256))
    return peer_experts(h2, u, v, st, stats, tb)


def kernel(x_prompt, x_sample, c_prompt, c_sample, cache_k, cache_v, page_table, state_conv, state_gdn, w_ada, b_ada, g_pre_mix, g_post_mix, g_pre_ffn, g_post_ffn, w_in, lam_q1, lam_k1, lam_q2, lam_k2, g_attn_head, conv_w, a_log, dt_bias, g_gdn_head, w_up_a, w_up_g, w_o, w_pq, sub_keys, peer_u, peer_v):
    depth = w_in.shape[0]
    bp, seq, _ = x_prompt.shape
    nb = x_sample.shape[0]
    yp = x_prompt.reshape(bp * seq, D_MODEL)
    ys = x_sample.reshape(nb, D_MODEL)
    outs = [[] for _ in range(8)]
    o_ga = 2 * A_W + A_W + 4 * G_W
    o_gate = o_ga + 2 * H_G
    for l in range(depth):
        lam_init = 0.8 - 0.6 * math.exp(-0.3 * l)
        wl = w_in[l]
        w_q = wl[:, 0:A_W].astype(BF16)
        w_k = wl[:, A_W:2 * A_W].astype(BF16)
        w_v = wl[:, 2 * A_W:3 * A_W].astype(BF16)
        w_g = wl[:, 3 * A_W:o_ga].astype(BF16)
        w_ab = jnp.pad(wl[:, o_ga:o_gate], ((0, 0), (0, LANES - 2 * H_G))).astype(BF16)
        w_gate = wl[:, o_gate:].astype(BF16)
        wua, wug = w_up_a[l].astype(BF16), w_up_g[l].astype(BF16)
        wo, wpq = w_o[l].astype(BF16), w_pq[l].astype(BF16)
        keys = sub_keys[l].astype(BF16)
        pu, pv = peer_u[l].astype(BF16), peer_v[l].astype(BF16)
        lam_vecs = jnp.stack([lam_q1[l], lam_k1[l], lam_q2[l], lam_k2[l]])

        mod = ada_mod(jnp.concatenate([c_prompt, c_sample], axis=0), w_ada[l], b_ada[l])
        mod_p, mod_s = mod[:bp], mod[bp:]

        tm = 1024
        h = prenorm(yp, g_pre_mix[l], mod_p, False, seq, 256)
        q = matmul(h, w_q, BF16, tm, 512)
        k = matmul(h, w_k, F32, tm, 512)
        v = matmul(h, w_v, F32, tm, 512)
        gproj = matmul(h, w_g, F32, tm, 512)
        gab = matmul(h, w_ab, F32, tm, LANES)
        gates = matmul(h, w_gate, F32, tm, 512)
        o_a = attn_prompt(q, k, v, lam_vecs, g_attn_head[l], bp, seq, lam_init)
        o_g, s_p = gdn_prompt(gproj, gab, conv_w[l], a_log[l], dt_bias[l], g_gdn_head[l], bp, seq)
        merged = merge(o_a, o_g, wua, wug, gates, tm)
        mix = matmul(merged, wo, F32, tm, 512)
        x1, h2 = postmix(yp, mix, g_post_mix[l], g_pre_ffn[l], mod_p, False, seq, 256)
        f = peer(h2, wpq, keys, pu, pv, 512)
        yp = final_residual(x1, f, g_post_ffn[l], mod_p, False, seq, 256)
        outs[0].append(k.reshape(bp, seq, H_A, HD_A))
        outs[1].append(v.reshape(bp, seq, H_A, HD_A))
        outs[2].append(gproj.reshape(bp, seq, 4 * G_W)[:, seq - (CONV_W - 1):, :C_CONV])
        outs[3].append(s_p)

        hs = prenorm(ys, g_pre_mix[l], mod_s, True, 1, nb)
        qs = matmul(hs, w_q, F32, nb, 512)
        ks = matmul(hs, w_k, F32, nb, 512)
        vs = matmul(hs, w_v, F32, nb, 512)
        gproj_s = matmul(hs, w_g, F32, nb, 512)
        gab_s = matmul(hs, w_ab, F32, nb, LANES)
        gates_s = matmul(hs, w_gate, F32, nb, 512)
        oa_s = attn_decode(qs, ks, vs, cache_k[l], cache_v[l], page_table, lam_vecs,
                           g_attn_head[l], lam_init)
        og_s, s_s = gdn_decode(gproj_s, gab_s, state_conv[l], conv_w[l], a_log[l], dt_bias[l],
                               g_gdn_head[l], state_gdn[l])
        merged_s = merge(oa_s.astype(BF16), og_s.astype(BF16), wua, wug, gates_s, nb)
        mix_s = matmul(merged_s, wo, F32, nb, 512)
        x1s, h2s = postmix(ys, mix_s, g_post_mix[l], g_pre_ffn[l], mod_s, True, 1, nb)
        h2s_pad = jnp.pad(h2s, ((0, LANES - nb), (0, 0)))
        fs = peer(h2s_pad, wpq, keys, pu, pv, LANES)[:nb]
        ys = final_residual(x1s, fs, g_post_ffn[l], mod_s, True, 1, nb)
        outs[4].append(ks.reshape(nb, 1, H_A, HD_A))
        outs[5].append(vs.reshape(nb, 1, H_A, HD_A))
        outs[6].append(jnp.concatenate([state_conv[l][:, 1:], gproj_s[:, None, :C_CONV]], axis=1))
        outs[7].append(s_s)

    return (yp.reshape(bp, seq, D_MODEL), ys.reshape(nb, 1, D_MODEL),
            *[jnp.stack(o) for o in outs])
```

```python
import functools
import math

import jax
import jax.numpy as jnp
from jax import lax
from jax.experimental import pallas as pl
from jax.experimental.pallas import tpu as pltpu

F32 = jnp.float32
BF16 = jnp.bfloat16

D_MODEL = 4096
H_A = 8
DH_A = 128
HD_A = 2 * DH_A
A_W = H_A * HD_A
H_G = 16
DK_G = 128
DV_G = 128
G_W = H_G * DK_G
CONV_W = 4
C_CONV = 3 * G_W
GDN_CHUNK = 128
GDN_INTERLEAVE = 8
N_KEYS = 128
N_EXPERTS = N_KEYS * N_KEYS
P_HEADS = 8
P_TOPK = 16
PAGE = 128
EPS = 1e-6
LANES = 128
MIB = 2 ** 20

NT_DIMS = (((1,), (1,)), ((), ()))
TN_DIMS = (((0,), (0,)), ((), ()))


def _cp(semantics, vmem_mib=48):
    return pltpu.CompilerParams(dimension_semantics=semantics, vmem_limit_bytes=vmem_mib * MIB)


def _dot(a, b, dims=None, precision=None):
    if dims is None:
        return jnp.dot(a, b, preferred_element_type=F32, precision=precision)
    return lax.dot_general(a, b, dims, preferred_element_type=F32, precision=precision)


def _silu(x):
    return x * jax.nn.sigmoid(x)


def _rms(x, g):
    return x * lax.rsqrt(jnp.mean(x * x, axis=-1, keepdims=True) + EPS) * g


def _mod_body(c_ref, w_ref, b_ref, o_ref):
    sc = _silu(c_ref[...]).astype(BF16)
    o_ref[...] = _dot(sc, w_ref[...].astype(BF16)) + b_ref[...]


def ada_mod(c_all, w_ada, b_ada, layer):
    r, d = c_all.shape
    n = w_ada.shape[2]
    tn = 512
    return pl.pallas_call(
        _mod_body,
        grid=(n // tn,),
        in_specs=[pl.BlockSpec((r, d), lambda j: (0, 0)),
                  pl.BlockSpec((None, d, tn), lambda j: (layer, 0, j)),
                  pl.BlockSpec((1, tn), lambda j: (0, j))],
        out_specs=pl.BlockSpec((r, tn), lambda j: (0, j)),
        out_shape=jax.ShapeDtypeStruct((r, n), F32),
        compiler_params=_cp(("parallel",)),
        name="ada_mod",
    )(c_all, w_ada, b_ada.reshape(1, n))


def _mod_spec(per_token, tm, seq, chunk):
    if per_token:
        return pl.BlockSpec((tm, D_MODEL), lambda i: (i, chunk))
    return pl.BlockSpec((None, None, 1, D_MODEL), lambda i: ((i * tm) // seq, chunk, 0, 0))


def _mod_arg(mod, per_token):
    return mod if per_token else mod.reshape(mod.shape[0], 6, 1, D_MODEL)


def _prenorm_body(x_ref, g_ref, sc_ref, sh_ref, o_ref):
    y = _rms(x_ref[...], g_ref[...])
    o_ref[...] = (y * (1 + sc_ref[...]) + sh_ref[...]).astype(o_ref.dtype)


def prenorm(x, g, mod, per_token, seq, tm):
    m = x.shape[0]
    ma = _mod_arg(mod, per_token)
    return pl.pallas_call(
        _prenorm_body,
        grid=(m // tm,),
        in_specs=[pl.BlockSpec((tm, D_MODEL), lambda i: (i, 0)),
                  pl.BlockSpec((1, D_MODEL), lambda i: (0, 0)),
                  _mod_spec(per_token, tm, seq, 1),
                  _mod_spec(per_token, tm, seq, 0)],
        out_specs=pl.BlockSpec((tm, D_MODEL), lambda i: (i, 0)),
        out_shape=jax.ShapeDtypeStruct((m, D_MODEL), BF16),
        compiler_params=_cp(("parallel",)),
        name="prenorm",
    )(x, g.reshape(1, D_MODEL), ma, ma)


def _mm_body(x_ref, w_ref, o_ref):
    o_ref[...] = _dot(x_ref[...], w_ref[...]).astype(o_ref.dtype)


def matmul(x, w, out_dtype, tm, tn):
    m, k = x.shape
    n = w.shape[1]
    tm, tn = min(tm, m), min(tn, n)
    return pl.pallas_call(
        _mm_body,
        grid=(m // tm, n // tn),
        in_specs=[pl.BlockSpec((tm, k), lambda i, j: (i, 0)),
                  pl.BlockSpec((k, tn), lambda i, j: (0, j))],
        out_specs=pl.BlockSpec((tm, tn), lambda i, j: (i, j)),
        out_shape=jax.ShapeDtypeStruct((m, n), out_dtype),
        compiler_params=_cp(("parallel", "parallel")),
        name="matmul",
    )(x, w)


ATTN_ROWS = 256


def _lam_value(lam_ref, lam_init):
    l = lam_ref[...]
    a = jnp.sum(l[0:1] * l[1:2], axis=-1, keepdims=True)
    b = jnp.sum(l[2:3] * l[3:4], axis=-1, keepdims=True)
    return jnp.exp(a) - jnp.exp(b) + lam_init


def _attn_body(lam_ref, q_ref, k_ref, v_ref, g_ref, o_ref, m_sc, l_sc, acc_sc, *, lam_init):
    qi = pl.program_id(2)
    ki = pl.program_id(3)
    tq, tk = q_ref.shape[0], k_ref.shape[0]
    scale = DH_A ** -0.5

    @pl.when(ki == 0)
    def _():
        m_sc[...] = jnp.full(m_sc.shape, -jnp.inf, F32)
        l_sc[...] = jnp.zeros(l_sc.shape, F32)
        acc_sc[...] = jnp.zeros(acc_sc.shape, F32)

    def update(diagonal):
        k = k_ref[...].astype(BF16)
        v = v_ref[...].astype(BF16)
        chains = [(g, c) for g in range(tq // ATTN_ROWS) for c in range(2)]
        rows = lambda g: slice(g * ATTN_ROWS, (g + 1) * ATTN_ROWS)
        lanes = lambda c: slice(c * DH_A, (c + 1) * DH_A)
        for g, c in chains:
            s = _dot(q_ref[rows(g), lanes(c)], k[:, lanes(c)], NT_DIMS) * scale
            if diagonal:
                row = lax.broadcasted_iota(jnp.int32, (ATTN_ROWS, tk), 0) + g * ATTN_ROWS
                col = lax.broadcasted_iota(jnp.int32, (ATTN_ROWS, tk), 1)
                s = jnp.where(col <= row, s, -jnp.inf)
            m_prev = m_sc[c, rows(g)]
            m_new = jnp.maximum(m_prev, jnp.max(s, axis=1, keepdims=True))
            alpha = jnp.exp(m_prev - m_new)
            m_sc[c, rows(g)] = m_new
            p = jnp.exp(s - m_new)
            l_sc[c, rows(g)] = alpha * l_sc[c, rows(g)] + jnp.sum(p, axis=1, keepdims=True)
            acc_sc[c, rows(g)] = alpha * acc_sc[c, rows(g)] + _dot(p.astype(BF16), v)

    @pl.when(ki < qi)
    def _():
        update(False)

    @pl.when(ki == qi)
    def _():
        update(True)

    @pl.when(ki == qi)
    def _():
        lam = _lam_value(lam_ref, lam_init)
        o = acc_sc[0] / l_sc[0] - lam * (acc_sc[1] / l_sc[1])
        o_ref[...] = (_rms(o, g_ref[...]) * (1 - lam_init)).astype(o_ref.dtype)


def attn_prompt(q, k, v, lam_vecs, g_head, batch, seq, lam_init, tq=512):
    nq = seq // tq
    kernel = functools.partial(_attn_body, lam_init=lam_init)
    return pl.pallas_call(
        kernel,
        grid=(batch, H_A, nq, nq),
        in_specs=[pl.BlockSpec((4, DH_A), lambda b, h, i, j: (0, 0)),
                  pl.BlockSpec((tq, HD_A), lambda b, h, i, j: (b * nq + i, h)),
                  pl.BlockSpec((tq, HD_A), lambda b, h, i, j: (b * nq + jnp.minimum(i, j), h)),
                  pl.BlockSpec((tq, HD_A), lambda b, h, i, j: (b * nq + jnp.minimum(i, j), h)),
                  pl.BlockSpec((1, HD_A), lambda b, h, i, j: (0, 0))],
        out_specs=pl.BlockSpec((tq, HD_A), lambda b, h, i, j: (b * nq + i, h)),
        out_shape=jax.ShapeDtypeStruct((batch * seq, A_W), BF16),
        scratch_shapes=[pltpu.VMEM((2, tq, 1), F32), pltpu.VMEM((2, tq, 1), F32),
                        pltpu.VMEM((2, tq, HD_A), F32)],
        compiler_params=_cp(("parallel", "parallel", "parallel", "arbitrary")),
        name="attn_prompt",
    )(lam_vecs, q, k, v, g_head.reshape(1, HD_A))


DEC_PAGES = 4


def _head_rows(ref):
    rowi = lax.broadcasted_iota(jnp.int32, (2 * H_A, HD_A), 0)
    out = jnp.zeros((2 * H_A, HD_A), F32)
    for h in range(H_A):
        piece = jnp.broadcast_to(ref[:, h * HD_A:(h + 1) * HD_A], (2 * H_A, HD_A))
        out = jnp.where(rowi // 2 == h, piece, out)
    return out


def _attn_dec_body(pt_ref, lam_ref, q_ref, kn_ref, vn_ref, g_ref, *refs, lam_init):
    kc_refs, vc_refs = refs[:DEC_PAGES], refs[DEC_PAGES:2 * DEC_PAGES]
    o_ref, m_sc, l_sc, acc_sc = refs[2 * DEC_PAGES:]
    step = pl.program_id(1)
    scale = DH_A ** -0.5
    nr = 2 * H_A
    rowi = lax.broadcasted_iota(jnp.int32, (nr, HD_A), 0)
    comp = lax.broadcasted_iota(jnp.int32, (nr, HD_A), 1) // DH_A
    q16 = jnp.where(comp == rowi % 2, _head_rows(q_ref), 0.0).astype(BF16)
    srow = lax.broadcasted_iota(jnp.int32, (nr, PAGE * H_A), 0)
    scol = lax.broadcasted_iota(jnp.int32, (nr, PAGE * H_A), 1)
    own_head = scol % H_A == srow // 2

    @pl.when(step == 0)
    def _():
        m_sc[...] = jnp.full(m_sc.shape, -jnp.inf, F32)
        l_sc[...] = jnp.zeros(l_sc.shape, F32)
        acc_sc[...] = jnp.zeros(acc_sc.shape, F32)

    s = [jnp.where(own_head, _dot(q16, kc_refs[i][...].astype(BF16), NT_DIMS) * scale, -jnp.inf)
         for i in range(DEC_PAGES)]
    m_page = s[0].max(axis=1, keepdims=True)
    for i in range(1, DEC_PAGES):
        m_page = jnp.maximum(m_page, s[i].max(axis=1, keepdims=True))
    m_prev = m_sc[...]
    m_new = jnp.maximum(m_prev, m_page)
    alpha = jnp.exp(m_prev - m_new)
    l_new = alpha * l_sc[...]
    acc = alpha * acc_sc[...]
    for i in range(DEC_PAGES):
        pr = jnp.exp(s[i] - m_new)
        l_new = l_new + jnp.sum(pr, axis=1, keepdims=True)
        acc = acc + _dot(pr.astype(BF16), vc_refs[i][...].astype(BF16))
    l_sc[...] = l_new
    acc_sc[...] = acc
    m_sc[...] = m_new

    @pl.when(step == pl.num_programs(1) - 1)
    def _():
        kn = _head_rows(kn_ref).astype(BF16).astype(F32)
        vn = _head_rows(vn_ref).astype(BF16).astype(F32)
        sn = jnp.sum(q16.astype(F32) * kn, axis=1, keepdims=True) * scale
        m_fin = jnp.maximum(m_new, sn)
        beta = jnp.exp(m_new - m_fin)
        pn = jnp.exp(sn - m_fin)
        o2 = (beta * acc + pn.astype(BF16).astype(F32) * vn) / (beta * l_new + pn)
        lam = _lam_value(lam_ref, lam_init)
        for h in range(H_A):
            o = o2[2 * h:2 * h + 1] - lam * o2[2 * h + 1:2 * h + 2]
            o_ref[:, h * HD_A:(h + 1) * HD_A] = _rms(o, g_ref[...]) * (1 - lam_init)


def attn_decode(q, k_new, v_new, cache_k, cache_v, page_table, lam_vecs, g_head, lam_init, layer):
    nb, n_pages = page_table.shape
    assert n_pages % DEC_PAGES == 0
    depth, pool = cache_k.shape[:2]
    kc = cache_k.reshape(depth, pool, PAGE * H_A, HD_A)
    vc = cache_v.reshape(depth, pool, PAGE * H_A, HD_A)
    row3 = lambda a: a.reshape(nb, 1, A_W)
    kernel = functools.partial(_attn_dec_body, lam_init=lam_init)
    tok = pl.BlockSpec((None, 1, A_W), lambda b, p, pt: (b, 0, 0))

    def page(i):
        return pl.BlockSpec((None, None, PAGE * H_A, HD_A),
                            lambda b, p, pt: (layer, pt[b * n_pages + p * DEC_PAGES + i], 0, 0))

    pages = [page(i) for i in range(DEC_PAGES)]
    out = pl.pallas_call(
        kernel,
        grid_spec=pltpu.PrefetchScalarGridSpec(
            num_scalar_prefetch=1,
            grid=(nb, n_pages // DEC_PAGES),
            in_specs=[pl.BlockSpec((4, DH_A), lambda b, p, pt: (0, 0)), tok, tok, tok,
                      pl.BlockSpec((1, HD_A), lambda b, p, pt: (0, 0))] + pages + pages,
            out_specs=tok,
            scratch_shapes=[pltpu.VMEM((2 * H_A, 1), F32), pltpu.VMEM((2 * H_A, 1), F32),
                            pltpu.VMEM((2 * H_A, HD_A), F32)]),
        out_shape=jax.ShapeDtypeStruct((nb, 1, A_W), F32),
        compiler_params=_cp(("parallel", "arbitrary")),
        name="attn_decode",
    )(page_table.reshape(-1), lam_vecs, row3(q), row3(k_new), row3(v_new),
      g_head.reshape(1, HD_A), *([kc] * DEC_PAGES), *([vc] * DEC_PAGES))
    return out.reshape(nb, A_W)


def _softplus(x):
    return jnp.maximum(x, 0.0) + jnp.log1p(jnp.exp(-jnp.abs(x)))


def _gate_columns(gab, al_ref, dtb_ref, h):
    lane = lax.broadcasted_iota(jnp.int32, gab.shape, 1)
    g_all = -jnp.exp(al_ref[...]) * _softplus(gab + dtb_ref[...])
    g = jnp.sum(jnp.where(lane == h, g_all, 0.0), axis=1, keepdims=True)
    beta = jnp.sum(jnp.where(lane == H_G + h, jax.nn.sigmoid(gab), 0.0), axis=1, keepdims=True)
    return g, beta


def _gdn_body(xq_ref, xk_ref, xv_ref, z_ref, wq_ref, wk_ref, wv_ref, gab_ref, al_ref, dtb_ref, gh_ref,
              o_ref, s_ref, q_sc, k_sc, v_sc, gc_sc, beta_sc, qe_sc, oc_sc, m_sc, b_sc, gl_sc):
    h = pl.program_id(1)
    t = xq_ref.shape[0]
    c_len = GDN_CHUNK
    n_chunks = t // c_len
    row = lax.broadcasted_iota(jnp.int32, (t, DK_G), 0)

    def conv(x_ref, w_ref):
        x = x_ref[...]
        w = w_ref[...]
        y = jnp.where(row >= 3, pltpu.roll(x, 3, 0), 0.0) * w[0:1]
        y = y + jnp.where(row >= 2, pltpu.roll(x, 2, 0), 0.0) * w[1:2]
        y = y + jnp.where(row >= 1, pltpu.roll(x, 1, 0), 0.0) * w[2:3]
        y = y + x * w[3:4]
        return _silu(y)

    cq = conv(xq_ref, wq_ref)
    q_sc[...] = cq * lax.rsqrt(jnp.sum(cq * cq, axis=-1, keepdims=True) + EPS) * (DK_G ** -0.5)
    ck = conv(xk_ref, wk_ref)
    k_sc[...] = ck * lax.rsqrt(jnp.sum(ck * ck, axis=-1, keepdims=True) + EPS)
    v_sc[...] = conv(xv_ref, wv_ref)

    g, beta = _gate_columns(gab_ref[...], al_ref, dtb_ref, h)
    beta_sc[...] = jnp.broadcast_to(beta, (t, DK_G))
    gc = jnp.broadcast_to(g, (t, DK_G))
    rc = row % c_len
    shift = 1
    while shift < c_len:
        gc = gc + jnp.where(rc >= shift, pltpu.roll(gc, shift, 0), 0.0)
        shift *= 2
    gc_sc[...] = gc

    ii = lax.broadcasted_iota(jnp.int32, (c_len, c_len), 0)
    jj = lax.broadcasted_iota(jnp.int32, (c_len, c_len), 1)

    def decay_of(gcc):
        return jnp.exp(jnp.where(ii >= jj, gcc - gcc.T, -jnp.inf))

    def chunk_group(gi, carry):
        rows = [pl.ds(pl.multiple_of((gi * GDN_INTERLEAVE + u) * c_len, c_len), c_len)
                for u in range(GDN_INTERLEAVE)]
        low = []
        for r in rows:
            kc = k_sc[r, :]
            kk = _dot((kc * beta_sc[r, :]).astype(BF16), kc.astype(BF16), NT_DIMS)
            low.append(jnp.where(ii > jj, kk * decay_of(gc_sc[r, :]), 0.0))
        y = [-jnp.where(ii // 2 == jj // 2, l, 0.0) for l in low]
        s = 2
        while s < c_len:
            in_pair = (ii // (2 * s) == jj // (2 * s)) & (ii // s != jj // s)
            off = [jnp.where(in_pair, l, 0.0) for l in low]
            g = [o + _dot(t.astype(BF16), o.astype(BF16)) for t, o in zip(y, off)]
            y = [t - a - _dot(a.astype(BF16), t.astype(BF16)) for t, a in zip(y, g)]
            s *= 2
        for u, r in enumerate(rows):
            c = gi * GDN_INTERLEAVE + u
            qc, kc, gcc, bc = q_sc[r, :], k_sc[r, :], gc_sc[r, :], beta_sc[r, :]
            eg = jnp.exp(gcc)
            vb = v_sc[r, :] * bc
            kbg = kc * bc * eg
            yb = y[u].astype(BF16)
            ub = (vb + _dot(yb, vb.astype(BF16))).astype(BF16)
            wb = (kbg + _dot(yb, kbg.astype(BF16))).astype(BF16)
            qk = _dot(qc.astype(BF16), kc.astype(BF16), NT_DIMS)
            attn = jnp.where(ii >= jj, qk * decay_of(gcc), 0.0).astype(BF16)
            g_last = gcc[c_len - 1:c_len, :]
            kd = (kc * jnp.exp(g_last - gcc)).astype(BF16)
            qe_sc[r, :] = (qc * eg - _dot(attn, wb)).astype(BF16)
            oc_sc[r, :] = _dot(attn, ub)
            m_sc[c] = _dot(kd, wb, TN_DIMS).astype(BF16)
            b_sc[c] = _dot(kd, ub, TN_DIMS)
            gl_sc[c] = jnp.exp(g_last)
        return carry

    lax.fori_loop(0, n_chunks // GDN_INTERLEAVE, chunk_group, 0)

    def scan_emit(c, s):
        rows = pl.ds(pl.multiple_of(c * c_len, c_len), c_len)
        sb = s.astype(BF16)
        o = _dot(qe_sc[rows, :], sb) + oc_sc[rows, :]
        o_ref[rows, :] = (_rms(o, gh_ref[...]) * _silu(z_ref[rows, :])).astype(o_ref.dtype)
        return s * gl_sc[c] - _dot(m_sc[c], sb) + b_sc[c]

    s_ref[...] = lax.fori_loop(0, n_chunks, scan_emit, jnp.zeros((DK_G, DV_G), F32), unroll=2)


def gdn_prompt(gproj, gab, conv_w, a_log, dt_bias, g_head, batch, seq):
    hd = lambda off: pl.BlockSpec((seq, DK_G), lambda b, h: (b, off + h))
    cw = lambda off: pl.BlockSpec((CONV_W, DK_G), lambda b, h: (0, off + h))
    vec = pl.BlockSpec((1, LANES), lambda b, h: (0, 0))
    assert seq % (GDN_CHUNK * GDN_INTERLEAVE) == 0
    nc = seq // GDN_CHUNK
    pad = lambda a: jnp.pad(a, (0, LANES - a.shape[0])).reshape(1, LANES)
    o, s = pl.pallas_call(
        _gdn_body,
        grid=(batch, H_G),
        in_specs=[hd(0), hd(H_G), hd(2 * H_G), hd(3 * H_G), cw(0), cw(H_G), cw(2 * H_G),
                  pl.BlockSpec((seq, LANES), lambda b, h: (b, 0)), vec, vec, vec],
        out_specs=[pl.BlockSpec((seq, DV_G), lambda b, h: (b, h)),
                   pl.BlockSpec((None, None, DK_G, DV_G), lambda b, h: (b, h, 0, 0))],
        out_shape=[jax.ShapeDtypeStruct((batch * seq, G_W), BF16),
                   jax.ShapeDtypeStruct((batch, H_G, DK_G, DV_G), F32)],
        scratch_shapes=[pltpu.VMEM((seq, DK_G), F32), pltpu.VMEM((seq, DK_G), F32),
                        pltpu.VMEM((seq, DV_G), F32), pltpu.VMEM((seq, DK_G), F32),
                        pltpu.VMEM((seq, DK_G), F32), pltpu.VMEM((seq, DK_G), BF16),
                        pltpu.VMEM((seq, DV_G), F32), pltpu.VMEM((nc, DK_G, DK_G), BF16),
                        pltpu.VMEM((nc, DK_G, DV_G), F32), pltpu.VMEM((nc, 1, LANES), F32)],
        compiler_params=_cp(("parallel", "parallel")),
        name="gdn_prompt",
    )(gproj, gproj, gproj, gproj, conv_w, conv_w, conv_w, gab, pad(a_log), pad(dt_bias),
      g_head.reshape(1, DV_G))
    return o, s


def _gdn_dec_body(x_ref, prev_ref, w_ref, gab_ref, al_ref, dtb_ref, gh_ref, s0_ref, o_ref, s_ref):
    hi = lax.Precision.HIGHEST
    x = x_ref[...]
    prev = prev_ref[...]
    w = w_ref[...]
    y = prev[0:1] * w[0:1]
    y = y + prev[1:2] * w[1:2]
    y = y + prev[2:3] * w[2:3]
    y = _silu(y + x[:, :C_CONV] * w[3:4])
    gab = gab_ref[...]
    g_all = -jnp.exp(al_ref[...]) * _softplus(gab + dtb_ref[...])
    b_all = jax.nn.sigmoid(gab)
    r8 = lax.broadcasted_iota(jnp.int32, (8, DK_G), 0)
    for h in range(H_G):
        sl = lambda base: slice(base + h * DK_G, base + (h + 1) * DK_G)
        cq, ck, v = y[:, sl(0)], y[:, sl(G_W)], y[:, sl(2 * G_W)]
        qn = cq * lax.rsqrt(jnp.sum(cq * cq, axis=-1, keepdims=True) + EPS) * (DK_G ** -0.5)
        kn = ck * lax.rsqrt(jnp.sum(ck * ck, axis=-1, keepdims=True) + EPS)
        eg = jnp.exp(g_all[:, h:h + 1])
        beta = b_all[:, H_G + h:H_G + h + 1]
        s0 = s0_ref[h]
        lhs = jnp.where(r8 == 0, jnp.broadcast_to(kn, (8, DK_G)),
                        jnp.where(r8 == 1, jnp.broadcast_to(qn, (8, DK_G)), 0.0))
        proj = _dot(lhs, s0, precision=hi)
        v_corr = beta * v - (beta * eg) * proj[0:1]
        o = eg * proj[1:2] + jnp.sum(qn * kn, axis=-1, keepdims=True) * v_corr
        k8 = jnp.where(r8 == 0, jnp.broadcast_to(kn, (8, DK_G)), 0.0)
        v8 = jnp.broadcast_to(v_corr, (8, DV_G))
        s_ref[h] = s0 * eg + _dot(k8, v8, TN_DIMS, precision=hi)
        z = x[:, sl(3 * G_W)]
        o_ref[:, h * DV_G:(h + 1) * DV_G] = _rms(o, gh_ref[...]) * _silu(z)


def gdn_decode(gproj, gab, conv_prev, conv_w, a_log, dt_bias, g_head, s0):
    nb = gproj.shape[0]
    vec = pl.BlockSpec((1, LANES), lambda b: (0, 0))
    pad = lambda a: jnp.pad(a, (0, LANES - a.shape[0])).reshape(1, LANES)
    o, s = pl.pallas_call(
        _gdn_dec_body,
        grid=(nb,),
        in_specs=[pl.BlockSpec((None, 1, 4 * G_W), lambda b: (b, 0, 0)),
                  pl.BlockSpec((None, CONV_W - 1, C_CONV), lambda b: (b, 0, 0)),
                  pl.BlockSpec((CONV_W, C_CONV), lambda b: (0, 0)),
                  pl.BlockSpec((None, 1, LANES), lambda b: (b, 0, 0)), vec, vec, vec,
                  pl.BlockSpec((None, H_G, DK_G, DV_G), lambda b: (b, 0, 0, 0))],
        out_specs=[pl.BlockSpec((None, 1, G_W), lambda b: (b, 0, 0)),
                   pl.BlockSpec((None, H_G, DK_G, DV_G), lambda b: (b, 0, 0, 0))],
        out_shape=[jax.ShapeDtypeStruct((nb, 1, G_W), F32),
                   jax.ShapeDtypeStruct((nb, H_G, DK_G, DV_G), F32)],
        compiler_params=_cp(("parallel",)),
        name="gdn_decode",
    )(gproj.reshape(nb, 1, 4 * G_W), conv_prev, conv_w, gab.reshape(nb, 1, LANES),
      pad(a_log), pad(dt_bias), g_head.reshape(1, DV_G), s0)
    return o.reshape(nb, G_W), s


def _merge_body(oa_ref, og_ref, wa_ref, wg_ref, ga_ref, gg_ref, o_ref):
    a = jax.nn.sigmoid(ga_ref[...]) * _dot(oa_ref[...], wa_ref[...])
    g = jax.nn.sigmoid(gg_ref[...]) * _dot(og_ref[...], wg_ref[...])
    o_ref[...] = (a + g).astype(o_ref.dtype)


def merge(o_a, o_g, w_up_a, w_up_g, gates, tm, tn=512):
    m = o_a.shape[0]
    tm = min(tm, m)
    nj = D_MODEL // tn
    return pl.pallas_call(
        _merge_body,
        grid=(m // tm, nj),
        in_specs=[pl.BlockSpec((tm, A_W), lambda i, j: (i, 0)),
                  pl.BlockSpec((tm, G_W), lambda i, j: (i, 0)),
                  pl.BlockSpec((A_W, tn), lambda i, j: (0, j)),
                  pl.BlockSpec((G_W, tn), lambda i, j: (0, j)),
                  pl.BlockSpec((tm, tn), lambda i, j: (i, j)),
                  pl.BlockSpec((tm, tn), lambda i, j: (i, nj + j))],
        out_specs=pl.BlockSpec((tm, tn), lambda i, j: (i, j)),
        out_shape=jax.ShapeDtypeStruct((m, D_MODEL), BF16),
        compiler_params=_cp(("parallel", "parallel")),
        name="merge",
    )(o_a, o_g, w_up_a, w_up_g, gates, gates)


def _postmix_body(x_ref, mix_ref, gpost_ref, gpre_ref, ga_ref, sc_ref, sh_ref, x1_ref, h2_ref):
    x1 = x_ref[...] + ga_ref[...] * _rms(mix_ref[...], gpost_ref[...])
    x1_ref[...] = x1
    h2_ref[...] = (_rms(x1, gpre_ref[...]) * (1 + sc_ref[...]) + sh_ref[...]).astype(h2_ref.dtype)


def postmix(x, mix, g_post, g_pre, mod, per_token, seq, tm):
    m = x.shape[0]
    ma = _mod_arg(mod, per_token)
    rowblk = pl.BlockSpec((tm, D_MODEL), lambda i: (i, 0))
    vec = pl.BlockSpec((1, D_MODEL), lambda i: (0, 0))
    return pl.pallas_call(
        _postmix_body,
        grid=(m // tm,),
        in_specs=[rowblk, rowblk, vec, vec, _mod_spec(per_token, tm, seq, 2),
                  _mod_spec(per_token, tm, seq, 4), _mod_spec(per_token, tm, seq, 3)],
        out_specs=[rowblk, rowblk],
        out_shape=[jax.ShapeDtypeStruct((m, D_MODEL), F32), jax.ShapeDtypeStruct((m, D_MODEL), BF16)],
        compiler_params=_cp(("parallel",)),
        name="postmix",
    )(x, mix, g_post.reshape(1, D_MODEL), g_pre.reshape(1, D_MODEL), ma, ma, ma)


def _final_body(x_ref, f_ref, g_ref, ga_ref, o_ref):
    o_ref[...] = x_ref[...] + ga_ref[...] * _rms(f_ref[...], g_ref[...])


def final_residual(x1, f, g_post, mod, per_token, seq, tm):
    m = x1.shape[0]
    ma = _mod_arg(mod, per_token)
    rowblk = pl.BlockSpec((tm, D_MODEL), lambda i: (i, 0))
    return pl.pallas_call(
        _final_body,
        grid=(m // tm,),
        in_specs=[rowblk, rowblk, pl.BlockSpec((1, D_MODEL), lambda i: (0, 0)),
                  _mod_spec(per_token, tm, seq, 5)],
        out_specs=rowblk,
        out_shape=jax.ShapeDtypeStruct((m, D_MODEL), F32),
        compiler_params=_cp(("parallel",)),
        name="final_residual",
    )(x1, f, g_post.reshape(1, D_MODEL), ma)


def _top_values(x, k):
    vals = []
    for _ in range(k):
        m = jnp.max(x, axis=0, keepdims=True)
        vals.append(m)
        x = jnp.where(x == m, -jnp.inf, x)
    return jnp.concatenate(vals, axis=0)


def _route_body(q_ref, keys_ref, st_ref, stats_ref):
    tb = q_ref.shape[0]
    for h in range(P_HEADS):
        tops = []
        for p in range(2):
            col = (2 * h + p) * N_KEYS
            s = _dot(keys_ref[h, p], q_ref[:, col:col + N_KEYS], NT_DIMS)
            st_ref[h, p] = s
            tops.append(_top_values(s, P_TOPK))
        v1, v2 = tops
        cand = jnp.concatenate([v1[a:a + 1] + v2 for a in range(P_TOPK)], axis=0)
        best = _top_values(cand, P_TOPK)
        z = jnp.sum(jnp.exp(best - best[0:1]), axis=0, keepdims=True)
        stats_ref[h] = jnp.concatenate(
            [best[P_TOPK - 1:P_TOPK], v1[0:1], v2[0:1], 1.0 / z, jnp.zeros((4, tb), F32)], axis=0)


def peer_route(qp, keys, tb):
    n = qp.shape[0]
    return pl.pallas_call(
        _route_body,
        grid=(n // tb,),
        in_specs=[pl.BlockSpec((tb, 2 * P_HEADS * N_KEYS), lambda i: (i, 0)),
                  pl.BlockSpec((P_HEADS, 2, N_KEYS, N_KEYS), lambda i: (0, 0, 0, 0))],
        out_specs=[pl.BlockSpec((P_HEADS, 2, N_KEYS, tb), lambda i: (0, 0, 0, i)),
                   pl.BlockSpec((P_HEADS, 8, tb), lambda i: (0, 0, i))],
        out_shape=[jax.ShapeDtypeStruct((P_HEADS, 2, N_KEYS, n), F32),
                   jax.ShapeDtypeStruct((P_HEADS, 8, n), F32)],
        compiler_params=_cp(("parallel",)),
        name="peer_route",
    )(qp, keys)


def _peer_body(h_ref, u_ref, v_ref, st_ref, stats_ref, o_ref):
    ei = pl.program_id(1)
    eb = u_ref.shape[0]
    rows_per_step = eb // N_KEYS

    @pl.when(ei == 0)
    def _():
        o_ref[...] = jnp.zeros(o_ref.shape, F32)

    a = _dot(u_ref[...], h_ref[...], NT_DIMS)
    act = 0.5 * a * (1.0 + lax.erf(a * (2.0 ** -0.5)))
    parts = []
    for r in range(rows_per_step):
        i = ei * rows_per_step + r
        wsum = None
        for h in range(P_HEADS):
            s1 = st_ref[h, 0, pl.ds(i, 1), :]
            c1 = jnp.exp(s1 - stats_ref[h, 1:2, :]) * stats_ref[h, 3:4, :]
            s2 = st_ref[h, 1]
            wgt = jnp.where(s1 + s2 >= stats_ref[h, 0:1, :], jnp.exp(s2 - stats_ref[h, 2:3, :]) * c1, 0.0)
            wsum = wgt if wsum is None else wsum + wgt
        parts.append((wsum * act[r * N_KEYS:(r + 1) * N_KEYS]).astype(BF16))
    wg = jnp.concatenate(parts, axis=0)
    o_ref[...] += _dot(wg, v_ref[...], TN_DIMS)


def peer_experts(h2, u, v, st, stats, tb, eb=512):
    n = h2.shape[0]
    return pl.pallas_call(
        _peer_body,
        grid=(n // tb, N_EXPERTS // eb),
        in_specs=[pl.BlockSpec((tb, D_MODEL), lambda i, e: (i, 0)),
                  pl.BlockSpec((eb, D_MODEL), lambda i, e: (e, 0)),
                  pl.BlockSpec((eb, D_MODEL), lambda i, e: (e, 0)),
                  pl.BlockSpec((P_HEADS, 2, N_KEYS, tb), lambda i, e: (0, 0, 0, i)),
                  pl.BlockSpec((P_HEADS, 8, tb), lambda i, e: (0, 0, i))],
        out_specs=pl.BlockSpec((tb, D_MODEL), lambda i, e: (i, 0)),
        out_shape=jax.ShapeDtypeStruct((n, D_MODEL), F32),
        compiler_params=_cp(("parallel", "arbitrary"), vmem_mib=56),
        name="peer_experts",
    )(h2, u, v, st, stats)


def peer(h2, w_pq, keys, u, v, tb):
    qp = matmul(h2, w_pq, BF16, 1024, 512)
    st, stats = peer_route(qp, keys, min(tb, 256))
    return peer_experts(h2, u, v, st, stats, tb)


def kernel(x_prompt, x_sample, c_prompt, c_sample, cache_k, cache_v, page_table, state_conv, state_gdn, w_ada, b_ada, g_pre_mix, g_post_mix, g_pre_ffn, g_post_ffn, w_in, lam_q1, lam_k1, lam_q2, lam_k2, g_attn_head, conv_w, a_log, dt_bias, g_gdn_head, w_up_a, w_up_g, w_o, w_pq, sub_keys, peer_u, peer_v):
    depth = w_in.shape[0]
    bp, seq, _ = x_prompt.shape
    nb = x_sample.shape[0]
    yp = x_prompt.reshape(bp * seq, D_MODEL)
    ys = x_sample.reshape(nb, D_MODEL)
    outs = [[] for _ in range(8)]
    o_ga = 2 * A_W + A_W + 4 * G_W
    o_gate = o_ga + 2 * H_G
    for l in range(depth):
        lam_init = 0.8 - 0.6 * math.exp(-0.3 * l)
        wl = w_in[l]
        w_q = wl[:, 0:A_W].astype(BF16)
        w_k = wl[:, A_W:2 * A_W].astype(BF16)
        w_v = wl[:, 2 * A_W:3 * A_W].astype(BF16)
        w_g = wl[:, 3 * A_W:o_ga].astype(BF16)
        w_ab = jnp.pad(wl[:, o_ga:o_gate], ((0, 0), (0, LANES - 2 * H_G))).astype(BF16)
        w_gate = wl[:, o_gate:].astype(BF16)
        wua, wug = w_up_a[l].astype(BF16), w_up_g[l].astype(BF16)
        wo, wpq = w_o[l].astype(BF16), w_pq[l].astype(BF16)
        keys = sub_keys[l].astype(BF16)
        pu, pv = peer_u[l].astype(BF16), peer_v[l].astype(BF16)
        lam_vecs = jnp.stack([lam_q1[l], lam_k1[l], lam_q2[l], lam_k2[l]])

        mod = ada_mod(jnp.concatenate([c_prompt, c_sample], axis=0), w_ada, b_ada[l], l)
        mod_p, mod_s = mod[:bp], mod[bp:]

        tm = 1024
        h = prenorm(yp, g_pre_mix[l], mod_p, False, seq, 256)
        q = matmul(h, w_q, BF16, tm, 512)
        k = matmul(h, w_k, F32, tm, 512)
        v = matmul(h, w_v, F32, tm, 512)
        gproj = matmul(h, w_g, F32, tm, 512)
        gab = matmul(h, w_ab, F32, tm, LANES)
        gates = matmul(h, w_gate, F32, tm, 512)
        o_a = attn_prompt(q, k, v, lam_vecs, g_attn_head[l], bp, seq, lam_init)
        o_g, s_p = gdn_prompt(gproj, gab, conv_w[l], a_log[l], dt_bias[l], g_gdn_head[l], bp, seq)
        merged = merge(o_a, o_g, wua, wug, gates, tm)
        mix = matmul(merged, wo, F32, tm, 512)
        x1, h2 = postmix(yp, mix, g_post_mix[l], g_pre_ffn[l], mod_p, False, seq, 256)
        f = peer(h2, wpq, keys, pu, pv, 512)
        yp = final_residual(x1, f, g_post_ffn[l], mod_p, False, seq, 256)
        outs[0].append(k.reshape(bp, seq, H_A, HD_A))
        outs[1].append(v.reshape(bp, seq, H_A, HD_A))
        outs[2].append(gproj.reshape(bp, seq, 4 * G_W)[:, seq - (CONV_W - 1):, :C_CONV])
        outs[3].append(s_p)

        hs = prenorm(ys, g_pre_mix[l], mod_s, True, 1, nb)
        qs = matmul(hs, w_q, F32, nb, 512)
        ks = matmul(hs, w_k, F32, nb, 512)
        vs = matmul(hs, w_v, F32, nb, 512)
        gproj_s = matmul(hs, w_g, F32, nb, 512)
        gab_s = matmul(hs, w_ab, F32, nb, LANES)
        gates_s = matmul(hs, w_gate, F32, nb, 512)
        oa_s = attn_decode(qs, ks, vs, cache_k, cache_v, page_table, lam_vecs,
                           g_attn_head[l], lam_init, l)
        og_s, s_s = gdn_decode(gproj_s, gab_s, state_conv[l], conv_w[l], a_log[l], dt_bias[l],
                               g_gdn_head[l], state_gdn[l])
        merged_s = merge(oa_s.astype(BF16), og_s.astype(BF16), wua, wug, gates_s, nb)
        mix_s = matmul(merged_s, wo, F32, nb, 512)
        x1s, h2s = postmix(ys, mix_s, g_post_mix[l], g_pre_ffn[l], mod_s, True, 1, nb)
        h2s_pad = jnp.pad(h2s, ((0, LANES - nb), (0, 0)))
        fs = peer(h2s_pad, wpq, keys, pu, pv, LANES)[:nb]
        ys = final_residual(x1s, fs, g_post_ffn[l], mod_s, True, 1, nb)
        outs[4].append(ks.reshape(nb, 1, H_A, HD_A))
        outs[5].append(vs.reshape(nb, 1, H_A, HD_A))
        outs[6].append(jnp.concatenate([state_conv[l][:, 1:], gproj_s[:, None, :C_CONV]], axis=1))
        outs[7].append(s_s)

    return (yp.reshape(bp, seq, D_MODEL), ys.reshape(nb, 1, D_MODEL),
            *[jnp.stack(o) for o in outs])
```

```python
import functools
import math

import jax
import jax.numpy as jnp
from jax import lax
from jax.experimental import pallas as pl
from jax.experimental.pallas import tpu as pltpu

F32 = jnp.float32
BF16 = jnp.bfloat16

D_MODEL = 4096
H_A = 8
DH_A = 128
HD_A = 2 * DH_A
A_W = H_A * HD_A
H_G = 16
DK_G = 128
DV_G = 128
G_W = H_G * DK_G
CONV_W = 4
C_CONV = 3 * G_W
GDN_CHUNK = 128
GDN_INTERLEAVE = 8
N_KEYS = 128
N_EXPERTS = N_KEYS * N_KEYS
P_HEADS = 8
P_TOPK = 16
PAGE = 128
EPS = 1e-6
LANES = 128
MIB = 2 ** 20

NT_DIMS = (((1,), (1,)), ((), ()))
TN_DIMS = (((0,), (0,)), ((), ()))


def _cp(semantics, vmem_mib=48):
    return pltpu.CompilerParams(dimension_semantics=semantics, vmem_limit_bytes=vmem_mib * MIB)


def _dot(a, b, dims=None, precision=None):
    if dims is None:
        return jnp.dot(a, b, preferred_element_type=F32, precision=precision)
    return lax.dot_general(a, b, dims, preferred_element_type=F32, precision=precision)


def _silu(x):
    return x * jax.nn.sigmoid(x)


def _rms(x, g):
    return x * lax.rsqrt(jnp.mean(x * x, axis=-1, keepdims=True) + EPS) * g


def _mod_body(c_ref, w_ref, b_ref, o_ref):
    sc = _silu(c_ref[...]).astype(BF16)
    o_ref[...] = _dot(sc, w_ref[...].astype(BF16)) + b_ref[...]


def ada_mod(c_all, w_ada, b_ada, layer):
    r, d = c_all.shape
    n = w_ada.shape[2]
    tn = 512
    return pl.pallas_call(
        _mod_body,
        grid=(n // tn,),
        in_specs=[pl.BlockSpec((r, d), lambda j: (0, 0)),
                  pl.BlockSpec((None, d, tn), lambda j: (layer, 0, j)),
                  pl.BlockSpec((1, tn), lambda j: (0, j))],
        out_specs=pl.BlockSpec((r, tn), lambda j: (0, j)),
        out_shape=jax.ShapeDtypeStruct((r, n), F32),
        compiler_params=_cp(("parallel",)),
        name="ada_mod",
    )(c_all, w_ada, b_ada.reshape(1, n))


def _mod_spec(per_token, tm, seq, chunk):
    if per_token:
        return pl.BlockSpec((tm, D_MODEL), lambda i: (i, chunk))
    return pl.BlockSpec((None, None, 1, D_MODEL), lambda i: ((i * tm) // seq, chunk, 0, 0))


def _mod_arg(mod, per_token):
    return mod if per_token else mod.reshape(mod.shape[0], 6, 1, D_MODEL)


def _prenorm_body(x_ref, g_ref, sc_ref, sh_ref, o_ref):
    y = _rms(x_ref[...], g_ref[...])
    o_ref[...] = (y * (1 + sc_ref[...]) + sh_ref[...]).astype(o_ref.dtype)


def prenorm(x, g, mod, per_token, seq, tm):
    m = x.shape[0]
    ma = _mod_arg(mod, per_token)
    return pl.pallas_call(
        _prenorm_body,
        grid=(m // tm,),
        in_specs=[pl.BlockSpec((tm, D_MODEL), lambda i: (i, 0)),
                  pl.BlockSpec((1, D_MODEL), lambda i: (0, 0)),
                  _mod_spec(per_token, tm, seq, 1),
                  _mod_spec(per_token, tm, seq, 0)],
        out_specs=pl.BlockSpec((tm, D_MODEL), lambda i: (i, 0)),
        out_shape=jax.ShapeDtypeStruct((m, D_MODEL), BF16),
        compiler_params=_cp(("parallel",)),
        name="prenorm",
    )(x, g.reshape(1, D_MODEL), ma, ma)


def _mm_body(x_ref, w_ref, o_ref):
    o_ref[...] = _dot(x_ref[...], w_ref[...]).astype(o_ref.dtype)


def matmul(x, w, out_dtype, tm, tn):
    m, k = x.shape
    n = w.shape[1]
    tm, tn = min(tm, m), min(tn, n)
    return pl.pallas_call(
        _mm_body,
        grid=(m // tm, n // tn),
        in_specs=[pl.BlockSpec((tm, k), lambda i, j: (i, 0)),
                  pl.BlockSpec((k, tn), lambda i, j: (0, j))],
        out_specs=pl.BlockSpec((tm, tn), lambda i, j: (i, j)),
        out_shape=jax.ShapeDtypeStruct((m, n), out_dtype),
        compiler_params=_cp(("parallel", "parallel")),
        name="matmul",
    )(x, w)


ATTN_ROWS = 256


def _lam_value(lam_ref, lam_init):
    l = lam_ref[...]
    a = jnp.sum(l[0:1] * l[1:2], axis=-1, keepdims=True)
    b = jnp.sum(l[2:3] * l[3:4], axis=-1, keepdims=True)
    return jnp.exp(a) - jnp.exp(b) + lam_init


def _attn_body(qi_ref, ki_ref, lam_ref, q_ref, k_ref, v_ref, g_ref, o_ref, m_sc, l_sc, acc_sc, *,
               lam_init):
    qi = qi_ref[pl.program_id(2)]
    ki = ki_ref[pl.program_id(2)]
    tq, tk = q_ref.shape[0], k_ref.shape[0]
    scale = DH_A ** -0.5

    @pl.when(ki == 0)
    def _():
        m_sc[...] = jnp.full(m_sc.shape, -jnp.inf, F32)
        l_sc[...] = jnp.zeros(l_sc.shape, F32)
        acc_sc[...] = jnp.zeros(acc_sc.shape, F32)

    def update(diagonal):
        k = k_ref[...].astype(BF16)
        v = v_ref[...].astype(BF16)
        chains = [(g, c) for g in range(tq // ATTN_ROWS) for c in range(2)]
        rows = lambda g: slice(g * ATTN_ROWS, (g + 1) * ATTN_ROWS)
        lanes = lambda c: slice(c * DH_A, (c + 1) * DH_A)
        for g, c in chains:
            s = _dot(q_ref[rows(g), lanes(c)], k[:, lanes(c)], NT_DIMS) * scale
            if diagonal:
                row = lax.broadcasted_iota(jnp.int32, (ATTN_ROWS, tk), 0) + g * ATTN_ROWS
                col = lax.broadcasted_iota(jnp.int32, (ATTN_ROWS, tk), 1)
                s = jnp.where(col <= row, s, -jnp.inf)
            m_prev = m_sc[c, rows(g)]
            m_new = jnp.maximum(m_prev, jnp.max(s, axis=1, keepdims=True))
            alpha = jnp.exp(m_prev - m_new)
            m_sc[c, rows(g)] = m_new
            p = jnp.exp(s - m_new)
            l_sc[c, rows(g)] = alpha * l_sc[c, rows(g)] + jnp.sum(p, axis=1, keepdims=True)
            acc_sc[c, rows(g)] = alpha * acc_sc[c, rows(g)] + _dot(p.astype(BF16), v)

    @pl.when(ki < qi)
    def _():
        update(False)

    @pl.when(ki == qi)
    def _():
        update(True)

    @pl.when(ki == qi)
    def _():
        lam = _lam_value(lam_ref, lam_init)
        o = acc_sc[0] / l_sc[0] - lam * (acc_sc[1] / l_sc[1])
        o_ref[...] = (_rms(o, g_ref[...]) * (1 - lam_init)).astype(o_ref.dtype)


def attn_prompt(q, k, v, lam_vecs, g_head, batch, seq, lam_init, tq=512):
    nq = seq // tq
    pairs = [(i, j) for i in range(nq) for j in range(i + 1)]
    qi_tab = jnp.asarray([p[0] for p in pairs], jnp.int32)
    ki_tab = jnp.asarray([p[1] for p in pairs], jnp.int32)
    kernel = functools.partial(_attn_body, lam_init=lam_init)
    q_blk = pl.BlockSpec((tq, HD_A), lambda b, h, t, qt, kt: (b * nq + qt[t], h))
    kv_blk = pl.BlockSpec((tq, HD_A), lambda b, h, t, qt, kt: (b * nq + kt[t], h))
    return pl.pallas_call(
        kernel,
        grid_spec=pltpu.PrefetchScalarGridSpec(
            num_scalar_prefetch=2,
            grid=(batch, H_A, len(pairs)),
            in_specs=[pl.BlockSpec((4, DH_A), lambda b, h, t, qt, kt: (0, 0)),
                      q_blk, kv_blk, kv_blk,
                      pl.BlockSpec((1, HD_A), lambda b, h, t, qt, kt: (0, 0))],
            out_specs=q_blk,
            scratch_shapes=[pltpu.VMEM((2, tq, 1), F32), pltpu.VMEM((2, tq, 1), F32),
                            pltpu.VMEM((2, tq, HD_A), F32)]),
        out_shape=jax.ShapeDtypeStruct((batch * seq, A_W), BF16),
        compiler_params=_cp(("parallel", "parallel", "arbitrary")),
        name="attn_prompt",
    )(qi_tab, ki_tab, lam_vecs, q, k, v, g_head.reshape(1, HD_A))


DEC_PAGES = 8


def _head_rows(ref):
    rowi = lax.broadcasted_iota(jnp.int32, (2 * H_A, HD_A), 0)
    out = jnp.zeros((2 * H_A, HD_A), F32)
    for h in range(H_A):
        piece = jnp.broadcast_to(ref[:, h * HD_A:(h + 1) * HD_A], (2 * H_A, HD_A))
        out = jnp.where(rowi // 2 == h, piece, out)
    return out


def _attn_dec_body(pt_ref, lam_ref, q_ref, kn_ref, vn_ref, g_ref, *refs, lam_init):
    kc_refs, vc_refs = refs[:DEC_PAGES], refs[DEC_PAGES:2 * DEC_PAGES]
    o_ref, m_sc, l_sc, acc_sc = refs[2 * DEC_PAGES:]
    step = pl.program_id(1)
    scale = DH_A ** -0.5
    nr = 2 * H_A
    rowi = lax.broadcasted_iota(jnp.int32, (nr, HD_A), 0)
    comp = lax.broadcasted_iota(jnp.int32, (nr, HD_A), 1) // DH_A
    q16 = jnp.where(comp == rowi % 2, _head_rows(q_ref), 0.0).astype(BF16)
    srow = lax.broadcasted_iota(jnp.int32, (nr, PAGE * H_A), 0)
    scol = lax.broadcasted_iota(jnp.int32, (nr, PAGE * H_A), 1)
    own_head = scol % H_A == srow // 2

    @pl.when(step == 0)
    def _():
        m_sc[...] = jnp.full(m_sc.shape, -jnp.inf, F32)
        l_sc[...] = jnp.zeros(l_sc.shape, F32)
        acc_sc[...] = jnp.zeros(acc_sc.shape, F32)

    s = [jnp.where(own_head, _dot(q16, kc_refs[i][...].astype(BF16), NT_DIMS) * scale, -jnp.inf)
         for i in range(DEC_PAGES)]
    m_page = s[0].max(axis=1, keepdims=True)
    for i in range(1, DEC_PAGES):
        m_page = jnp.maximum(m_page, s[i].max(axis=1, keepdims=True))
    m_prev = m_sc[...]
    m_new = jnp.maximum(m_prev, m_page)
    alpha = jnp.exp(m_prev - m_new)
    l_new = alpha * l_sc[...]
    acc = alpha * acc_sc[...]
    for i in range(DEC_PAGES):
        pr = jnp.exp(s[i] - m_new)
        l_new = l_new + jnp.sum(pr, axis=1, keepdims=True)
        acc = acc + _dot(pr.astype(BF16), vc_refs[i][...].astype(BF16))
    l_sc[...] = l_new
    acc_sc[...] = acc
    m_sc[...] = m_new

    @pl.when(step == pl.num_programs(1) - 1)
    def _():
        kn = _head_rows(kn_ref).astype(BF16).astype(F32)
        vn = _head_rows(vn_ref).astype(BF16).astype(F32)
        sn = jnp.sum(q16.astype(F32) * kn, axis=1, keepdims=True) * scale
        m_fin = jnp.maximum(m_new, sn)
        beta = jnp.exp(m_new - m_fin)
        pn = jnp.exp(sn - m_fin)
        o2 = (beta * acc + pn.astype(BF16).astype(F32) * vn) / (beta * l_new + pn)
        lam = _lam_value(lam_ref, lam_init)
        for h in range(H_A):
            o = o2[2 * h:2 * h + 1] - lam * o2[2 * h + 1:2 * h + 2]
            o_ref[:, h * HD_A:(h + 1) * HD_A] = _rms(o, g_ref[...]) * (1 - lam_init)


def attn_decode(q, k_new, v_new, cache_k, cache_v, page_table, lam_vecs, g_head, lam_init, layer):
    nb, n_pages = page_table.shape
    assert n_pages % DEC_PAGES == 0
    depth, pool = cache_k.shape[:2]
    kc = cache_k.reshape(depth, pool, PAGE * H_A, HD_A)
    vc = cache_v.reshape(depth, pool, PAGE * H_A, HD_A)
    row3 = lambda a: a.reshape(nb, 1, A_W)
    kernel = functools.partial(_attn_dec_body, lam_init=lam_init)
    tok = pl.BlockSpec((None, 1, A_W), lambda b, p, pt: (b, 0, 0))

    def page(i):
        return pl.BlockSpec((None, None, PAGE * H_A, HD_A),
                            lambda b, p, pt: (layer, pt[b * n_pages + p * DEC_PAGES + i], 0, 0))

    pages = [page(i) for i in range(DEC_PAGES)]
    out = pl.pallas_call(
        kernel,
        grid_spec=pltpu.PrefetchScalarGridSpec(
            num_scalar_prefetch=1,
            grid=(nb, n_pages // DEC_PAGES),
            in_specs=[pl.BlockSpec((4, DH_A), lambda b, p, pt: (0, 0)), tok, tok, tok,
                      pl.BlockSpec((1, HD_A), lambda b, p, pt: (0, 0))] + pages + pages,
            out_specs=tok,
            scratch_shapes=[pltpu.VMEM((2 * H_A, 1), F32), pltpu.VMEM((2 * H_A, 1), F32),
                            pltpu.VMEM((2 * H_A, HD_A), F32)]),
        out_shape=jax.ShapeDtypeStruct((nb, 1, A_W), F32),
        compiler_params=_cp(("parallel", "arbitrary")),
        name="attn_decode",
    )(page_table.reshape(-1), lam_vecs, row3(q), row3(k_new), row3(v_new),
      g_head.reshape(1, HD_A), *([kc] * DEC_PAGES), *([vc] * DEC_PAGES))
    return out.reshape(nb, A_W)


def _softplus(x):
    return jnp.maximum(x, 0.0) + jnp.log1p(jnp.exp(-jnp.abs(x)))


def _gate_columns(gab, al_ref, dtb_ref, h):
    lane = lax.broadcasted_iota(jnp.int32, gab.shape, 1)
    g_all = -jnp.exp(al_ref[...]) * _softplus(gab + dtb_ref[...])
    g = jnp.sum(jnp.where(lane == h, g_all, 0.0), axis=1, keepdims=True)
    beta = jnp.sum(jnp.where(lane == H_G + h, jax.nn.sigmoid(gab), 0.0), axis=1, keepdims=True)
    return g, beta


def _gdn_body(xq_ref, xk_ref, xv_ref, z_ref, wq_ref, wk_ref, wv_ref, gab_ref, al_ref, dtb_ref, gh_ref,
              o_ref, s_ref, q_sc, k_sc, v_sc, gc_sc, beta_sc, qe_sc, oc_sc, m_sc, b_sc, gl_sc):
    h = pl.program_id(1)
    t = xq_ref.shape[0]
    c_len = GDN_CHUNK
    n_chunks = t // c_len
    row = lax.broadcasted_iota(jnp.int32, (t, DK_G), 0)

    def conv(x_ref, w_ref):
        x = x_ref[...]
        w = w_ref[...]
        y = jnp.where(row >= 3, pltpu.roll(x, 3, 0), 0.0) * w[0:1]
        y = y + jnp.where(row >= 2, pltpu.roll(x, 2, 0), 0.0) * w[1:2]
        y = y + jnp.where(row >= 1, pltpu.roll(x, 1, 0), 0.0) * w[2:3]
        y = y + x * w[3:4]
        return _silu(y)

    cq = conv(xq_ref, wq_ref)
    q_sc[...] = cq * lax.rsqrt(jnp.sum(cq * cq, axis=-1, keepdims=True) + EPS) * (DK_G ** -0.5)
    ck = conv(xk_ref, wk_ref)
    k_sc[...] = ck * lax.rsqrt(jnp.sum(ck * ck, axis=-1, keepdims=True) + EPS)
    v_sc[...] = conv(xv_ref, wv_ref)

    g, beta = _gate_columns(gab_ref[...], al_ref, dtb_ref, h)
    beta_sc[...] = jnp.broadcast_to(beta, (t, DK_G))
    gc = jnp.broadcast_to(g, (t, DK_G))
    rc = row % c_len
    shift = 1
    while shift < c_len:
        gc = gc + jnp.where(rc >= shift, pltpu.roll(gc, shift, 0), 0.0)
        shift *= 2
    gc_sc[...] = gc

    ii = lax.broadcasted_iota(jnp.int32, (c_len, c_len), 0)
    jj = lax.broadcasted_iota(jnp.int32, (c_len, c_len), 1)

    def decay_of(gcc):
        return jnp.exp(jnp.where(ii >= jj, gcc - gcc.T, -jnp.inf))

    def chunk_group(gi, carry):
        rows = [pl.ds(pl.multiple_of((gi * GDN_INTERLEAVE + u) * c_len, c_len), c_len)
                for u in range(GDN_INTERLEAVE)]
        low = []
        for r in rows:
            kc = k_sc[r, :]
            kk = _dot((kc * beta_sc[r, :]).astype(BF16), kc.astype(BF16), NT_DIMS)
            low.append(jnp.where(ii > jj, kk * decay_of(gc_sc[r, :]), 0.0))
        y = [-jnp.where(ii // 2 == jj // 2, l, 0.0) for l in low]
        s = 2
        while s < c_len:
            in_pair = (ii // (2 * s) == jj // (2 * s)) & (ii // s != jj // s)
            off = [jnp.where(in_pair, l, 0.0) for l in low]
            g = [o + _dot(t.astype(BF16), o.astype(BF16)) for t, o in zip(y, off)]
            y = [t - a - _dot(a.astype(BF16), t.astype(BF16)) for t, a in zip(y, g)]
            s *= 2
        for u, r in enumerate(rows):
            c = gi * GDN_INTERLEAVE + u
            qc, kc, gcc, bc = q_sc[r, :], k_sc[r, :], gc_sc[r, :], beta_sc[r, :]
            eg = jnp.exp(gcc)
            vb = v_sc[r, :] * bc
            kbg = kc * bc * eg
            yb = y[u].astype(BF16)
            ub = (vb + _dot(yb, vb.astype(BF16))).astype(BF16)
            wb = (kbg + _dot(yb, kbg.astype(BF16))).astype(BF16)
            qk = _dot(qc.astype(BF16), kc.astype(BF16), NT_DIMS)
            attn = jnp.where(ii >= jj, qk * decay_of(gcc), 0.0).astype(BF16)
            g_last = gcc[c_len - 1:c_len, :]
            kd = (kc * jnp.exp(g_last - gcc)).astype(BF16)
            qe_sc[r, :] = (qc * eg - _dot(attn, wb)).astype(BF16)
            oc_sc[r, :] = _dot(attn, ub)
            m_sc[c] = _dot(kd, wb, TN_DIMS).astype(BF16)
            b_sc[c] = _dot(kd, ub, TN_DIMS)
            gl_sc[c] = jnp.exp(g_last)
        return carry

    lax.fori_loop(0, n_chunks // GDN_INTERLEAVE, chunk_group, 0)

    def scan_emit(c, s):
        rows = pl.ds(pl.multiple_of(c * c_len, c_len), c_len)
        sb = s.astype(BF16)
        o = _dot(qe_sc[rows, :], sb) + oc_sc[rows, :]
        o_ref[rows, :] = (_rms(o, gh_ref[...]) * _silu(z_ref[rows, :])).astype(o_ref.dtype)
        return s * gl_sc[c] - _dot(m_sc[c], sb) + b_sc[c]

    s_ref[...] = lax.fori_loop(0, n_chunks, scan_emit, jnp.zeros((DK_G, DV_G), F32), unroll=2)


def gdn_prompt(gproj, gab, conv_w, a_log, dt_bias, g_head, batch, seq):
    hd = lambda off: pl.BlockSpec((seq, DK_G), lambda b, h: (b, off + h))
    cw = lambda off: pl.BlockSpec((CONV_W, DK_G), lambda b, h: (0, off + h))
    vec = pl.BlockSpec((1, LANES), lambda b, h: (0, 0))
    assert seq % (GDN_CHUNK * GDN_INTERLEAVE) == 0
    nc = seq // GDN_CHUNK
    pad = lambda a: jnp.pad(a, (0, LANES - a.shape[0])).reshape(1, LANES)
    o, s = pl.pallas_call(
        _gdn_body,
        grid=(batch, H_G),
        in_specs=[hd(0), hd(H_G), hd(2 * H_G), hd(3 * H_G), cw(0), cw(H_G), cw(2 * H_G),
                  pl.BlockSpec((seq, LANES), lambda b, h: (b, 0)), vec, vec, vec],
        out_specs=[pl.BlockSpec((seq, DV_G), lambda b, h: (b, h)),
                   pl.BlockSpec((None, None, DK_G, DV_G), lambda b, h: (b, h, 0, 0))],
        out_shape=[jax.ShapeDtypeStruct((batch * seq, G_W), BF16),
                   jax.ShapeDtypeStruct((batch, H_G, DK_G, DV_G), F32)],
        scratch_shapes=[pltpu.VMEM((seq, DK_G), F32), pltpu.VMEM((seq, DK_G), F32),
                        pltpu.VMEM((seq, DV_G), F32), pltpu.VMEM((seq, DK_G), F32),
                        pltpu.VMEM((seq, DK_G), F32), pltpu.VMEM((seq, DK_G), BF16),
                        pltpu.VMEM((seq, DV_G), F32), pltpu.VMEM((nc, DK_G, DK_G), BF16),
                        pltpu.VMEM((nc, DK_G, DV_G), F32), pltpu.VMEM((nc, 1, LANES), F32)],
        compiler_params=_cp(("parallel", "parallel")),
        name="gdn_prompt",
    )(gproj, gproj, gproj, gproj, conv_w, conv_w, conv_w, gab, pad(a_log), pad(dt_bias),
      g_head.reshape(1, DV_G))
    return o, s


def _gdn_dec_body(x_ref, prev_ref, w_ref, gab_ref, al_ref, dtb_ref, gh_ref, s0_ref, o_ref, s_ref):
    hi = lax.Precision.HIGHEST
    x = x_ref[...]
    prev = prev_ref[...]
    w = w_ref[...]
    y = prev[0:1] * w[0:1]
    y = y + prev[1:2] * w[1:2]
    y = y + prev[2:3] * w[2:3]
    y = _silu(y + x[:, :C_CONV] * w[3:4])
    gab = gab_ref[...]
    g_all = -jnp.exp(al_ref[...]) * _softplus(gab + dtb_ref[...])
    b_all = jax.nn.sigmoid(gab)
    r8 = lax.broadcasted_iota(jnp.int32, (8, DK_G), 0)
    for h in range(H_G):
        sl = lambda base: slice(base + h * DK_G, base + (h + 1) * DK_G)
        cq, ck, v = y[:, sl(0)], y[:, sl(G_W)], y[:, sl(2 * G_W)]
        qn = cq * lax.rsqrt(jnp.sum(cq * cq, axis=-1, keepdims=True) + EPS) * (DK_G ** -0.5)
        kn = ck * lax.rsqrt(jnp.sum(ck * ck, axis=-1, keepdims=True) + EPS)
        eg = jnp.exp(g_all[:, h:h + 1])
        beta = b_all[:, H_G + h:H_G + h + 1]
        s0 = s0_ref[h]
        lhs = jnp.where(r8 == 0, jnp.broadcast_to(kn, (8, DK_G)),
                        jnp.where(r8 == 1, jnp.broadcast_to(qn, (8, DK_G)), 0.0))
        proj = _dot(lhs, s0, precision=hi)
        v_corr = beta * v - (beta * eg) * proj[0:1]
        o = eg * proj[1:2] + jnp.sum(qn * kn, axis=-1, keepdims=True) * v_corr
        k8 = jnp.where(r8 == 0, jnp.broadcast_to(kn, (8, DK_G)), 0.0)
        v8 = jnp.broadcast_to(v_corr, (8, DV_G))
        s_ref[h] = s0 * eg + _dot(k8, v8, TN_DIMS, precision=hi)
        z = x[:, sl(3 * G_W)]
        o_ref[:, h * DV_G:(h + 1) * DV_G] = _rms(o, gh_ref[...]) * _silu(z)


def gdn_decode(gproj, gab, conv_prev, conv_w, a_log, dt_bias, g_head, s0):
    nb = gproj.shape[0]
    vec = pl.BlockSpec((1, LANES), lambda b: (0, 0))
    pad = lambda a: jnp.pad(a, (0, LANES - a.shape[0])).reshape(1, LANES)
    o, s = pl.pallas_call(
        _gdn_dec_body,
        grid=(nb,),
        in_specs=[pl.BlockSpec((None, 1, 4 * G_W), lambda b: (b, 0, 0)),
                  pl.BlockSpec((None, CONV_W - 1, C_CONV), lambda b: (b, 0, 0)),
                  pl.BlockSpec((CONV_W, C_CONV), lambda b: (0, 0)),
                  pl.BlockSpec((None, 1, LANES), lambda b: (b, 0, 0)), vec, vec, vec,
                  pl.BlockSpec((None, H_G, DK_G, DV_G), lambda b: (b, 0, 0, 0))],
        out_specs=[pl.BlockSpec((None, 1, G_W), lambda b: (b, 0, 0)),
                   pl.BlockSpec((None, H_G, DK_G, DV_G), lambda b: (b, 0, 0, 0))],
        out_shape=[jax.ShapeDtypeStruct((nb, 1, G_W), F32),
                   jax.ShapeDtypeStruct((nb, H_G, DK_G, DV_G), F32)],
        compiler_params=_cp(("parallel",)),
        name="gdn_decode",
    )(gproj.reshape(nb, 1, 4 * G_W), conv_prev, conv_w, gab.reshape(nb, 1, LANES),
      pad(a_log), pad(dt_bias), g_head.reshape(1, DV_G), s0)
    return o.reshape(nb, G_W), s


def _merge_body(oa_ref, og_ref, wa_ref, wg_ref, ga_ref, gg_ref, o_ref):
    a = jax.nn.sigmoid(ga_ref[...]) * _dot(oa_ref[...], wa_ref[...])
    g = jax.nn.sigmoid(gg_ref[...]) * _dot(og_ref[...], wg_ref[...])
    o_ref[...] = (a + g).astype(o_ref.dtype)


def merge(o_a, o_g, w_up_a, w_up_g, gates, tm, tn=512):
    m = o_a.shape[0]
    tm = min(tm, m)
    nj = D_MODEL // tn
    return pl.pallas_call(
        _merge_body,
        grid=(m // tm, nj),
        in_specs=[pl.BlockSpec((tm, A_W), lambda i, j: (i, 0)),
                  pl.BlockSpec((tm, G_W), lambda i, j: (i, 0)),
                  pl.BlockSpec((A_W, tn), lambda i, j: (0, j)),
                  pl.BlockSpec((G_W, tn), lambda i, j: (0, j)),
                  pl.BlockSpec((tm, tn), lambda i, j: (i, j)),
                  pl.BlockSpec((tm, tn), lambda i, j: (i, nj + j))],
        out_specs=pl.BlockSpec((tm, tn), lambda i, j: (i, j)),
        out_shape=jax.ShapeDtypeStruct((m, D_MODEL), BF16),
        compiler_params=_cp(("parallel", "parallel")),
        name="merge",
    )(o_a, o_g, w_up_a, w_up_g, gates, gates)


def _postmix_body(x_ref, mix_ref, gpost_ref, gpre_ref, ga_ref, sc_ref, sh_ref, x1_ref, h2_ref):
    x1 = x_ref[...] + ga_ref[...] * _rms(mix_ref[...], gpost_ref[...])
    x1_ref[...] = x1
    h2_ref[...] = (_rms(x1, gpre_ref[...]) * (1 + sc_ref[...]) + sh_ref[...]).astype(h2_ref.dtype)


def postmix(x, mix, g_post, g_pre, mod, per_token, seq, tm):
    m = x.shape[0]
    ma = _mod_arg(mod, per_token)
    rowblk = pl.BlockSpec((tm, D_MODEL), lambda i: (i, 0))
    vec = pl.BlockSpec((1, D_MODEL), lambda i: (0, 0))
    return pl.pallas_call(
        _postmix_body,
        grid=(m // tm,),
        in_specs=[rowblk, rowblk, vec, vec, _mod_spec(per_token, tm, seq, 2),
                  _mod_spec(per_token, tm, seq, 4), _mod_spec(per_token, tm, seq, 3)],
        out_specs=[rowblk, rowblk],
        out_shape=[jax.ShapeDtypeStruct((m, D_MODEL), F32), jax.ShapeDtypeStruct((m, D_MODEL), BF16)],
        compiler_params=_cp(("parallel",)),
        name="postmix",
    )(x, mix, g_post.reshape(1, D_MODEL), g_pre.reshape(1, D_MODEL), ma, ma, ma)


def _final_body(x_ref, f_ref, g_ref, ga_ref, o_ref):
    o_ref[...] = x_ref[...] + ga_ref[...] * _rms(f_ref[...], g_ref[...])


def final_residual(x1, f, g_post, mod, per_token, seq, tm):
    m = x1.shape[0]
    ma = _mod_arg(mod, per_token)
    rowblk = pl.BlockSpec((tm, D_MODEL), lambda i: (i, 0))
    return pl.pallas_call(
        _final_body,
        grid=(m // tm,),
        in_specs=[rowblk, rowblk, pl.BlockSpec((1, D_MODEL), lambda i: (0, 0)),
                  _mod_spec(per_token, tm, seq, 5)],
        out_specs=rowblk,
        out_shape=jax.ShapeDtypeStruct((m, D_MODEL), F32),
        compiler_params=_cp(("parallel",)),
        name="final_residual",
    )(x1, f, g_post.reshape(1, D_MODEL), ma)


def _top_values(x, k):
    vals = []
    for _ in range(k):
        m = jnp.max(x, axis=0, keepdims=True)
        vals.append(m)
        x = jnp.where(x == m, -jnp.inf, x)
    return jnp.concatenate(vals, axis=0)


def _route_body(q_ref, keys_ref, st_ref, stats_ref):
    tb = q_ref.shape[0]
    for h in range(P_HEADS):
        tops = []
        for p in range(2):
            col = (2 * h + p) * N_KEYS
            s = _dot(keys_ref[h, p], q_ref[:, col:col + N_KEYS], NT_DIMS)
            st_ref[h, p] = s
            tops.append(_top_values(s, P_TOPK))
        v1, v2 = tops
        cand = jnp.concatenate([v1[a:a + 1] + v2[0:P_TOPK // (a + 1)] for a in range(P_TOPK)], axis=0)
        best = _top_values(cand, P_TOPK)
        z = jnp.sum(jnp.exp(best - best[0:1]), axis=0, keepdims=True)
        stats_ref[h] = jnp.concatenate(
            [best[P_TOPK - 1:P_TOPK], v1[0:1], v2[0:1], 1.0 / z, jnp.zeros((4, tb), F32)], axis=0)


def peer_route(qp, keys, tb):
    n = qp.shape[0]
    return pl.pallas_call(
        _route_body,
        grid=(n // tb,),
        in_specs=[pl.BlockSpec((tb, 2 * P_HEADS * N_KEYS), lambda i: (i, 0)),
                  pl.BlockSpec((P_HEADS, 2, N_KEYS, N_KEYS), lambda i: (0, 0, 0, 0))],
        out_specs=[pl.BlockSpec((P_HEADS, 2, N_KEYS, tb), lambda i: (0, 0, 0, i)),
                   pl.BlockSpec((P_HEADS, 8, tb), lambda i: (0, 0, i))],
        out_shape=[jax.ShapeDtypeStruct((P_HEADS, 2, N_KEYS, n), F32),
                   jax.ShapeDtypeStruct((P_HEADS, 8, n), F32)],
        compiler_params=_cp(("parallel",)),
        name="peer_route",
    )(qp, keys)


def _peer_body(h_ref, u_ref, v_ref, st_ref, stats_ref, o_ref):
    ei = pl.program_id(1)
    eb = u_ref.shape[0]
    rows_per_step = eb // N_KEYS

    @pl.when(ei == 0)
    def _():
        o_ref[...] = jnp.zeros(o_ref.shape, F32)

    a = _dot(u_ref[...], h_ref[...], NT_DIMS)
    act = 0.5 * a * (1.0 + lax.erf(a * (2.0 ** -0.5)))
    parts = []
    for r in range(rows_per_step):
        i = ei * rows_per_step + r
        wsum = None
        for h in range(P_HEADS):
            s1 = st_ref[h, 0, pl.ds(i, 1), :]
            c1 = jnp.exp(s1 - stats_ref[h, 1:2, :]) * stats_ref[h, 3:4, :]
            s2 = st_ref[h, 1]
            wgt = jnp.where(s1 + s2 >= stats_ref[h, 0:1, :], jnp.exp(s2 - stats_ref[h, 2:3, :]) * c1, 0.0)
            wsum = wgt if wsum is None else wsum + wgt
        parts.append((wsum * act[r * N_KEYS:(r + 1) * N_KEYS]).astype(BF16))
    wg = jnp.concatenate(parts, axis=0)
    o_ref[...] += _dot(wg, v_ref[...], TN_DIMS)


def peer_experts(h2, u, v, st, stats, tb, eb=512):
    n = h2.shape[0]
    return pl.pallas_call(
        _peer_body,
        grid=(n // tb, N_EXPERTS // eb),
        in_specs=[pl.BlockSpec((tb, D_MODEL), lambda i, e: (i, 0)),
                  pl.BlockSpec((eb, D_MODEL), lambda i, e: (e, 0)),
                  pl.BlockSpec((eb, D_MODEL), lambda i, e: (e, 0)),
                  pl.BlockSpec((P_HEADS, 2, N_KEYS, tb), lambda i, e: (0, 0, 0, i)),
                  pl.BlockSpec((P_HEADS, 8, tb), lambda i, e: (0, 0, i))],
        out_specs=pl.BlockSpec((tb, D_MODEL), lambda i, e: (i, 0)),
        out_shape=jax.ShapeDtypeStruct((n, D_MODEL), F32),
        compiler_params=_cp(("parallel", "arbitrary"), vmem_mib=56),
        name="peer_experts",
    )(h2, u, v, st, stats)


def peer(h2, w_pq, keys, u, v, tb):
    qp = matmul(h2, w_pq, BF16, 1024, 512)
    st, stats = peer_route(qp, keys, min(tb, 256))
    return peer_experts(h2, u, v, st, stats, tb)


def kernel(x_prompt, x_sample, c_prompt, c_sample, cache_k, cache_v, page_table, state_conv, state_gdn, w_ada, b_ada, g_pre_mix, g_post_mix, g_pre_ffn, g_post_ffn, w_in, lam_q1, lam_k1, lam_q2, lam_k2, g_attn_head, conv_w, a_log, dt_bias, g_gdn_head, w_up_a, w_up_g, w_o, w_pq, sub_keys, peer_u, peer_v):
    depth = w_in.shape[0]
    bp, seq, _ = x_prompt.shape
    nb = x_sample.shape[0]
    yp = x_prompt.reshape(bp * seq, D_MODEL)
    ys = x_sample.reshape(nb, D_MODEL)
    outs = [[] for _ in range(8)]
    o_ga = 2 * A_W + A_W + 4 * G_W
    o_gate = o_ga + 2 * H_G
    for l in range(depth):
        lam_init = 0.8 - 0.6 * math.exp(-0.3 * l)
        wl = w_in[l]
        w_q = wl[:, 0:A_W].astype(BF16)
        w_k = wl[:, A_W:2 * A_W].astype(BF16)
        w_v = wl[:, 2 * A_W:3 * A_W].astype(BF16)
        w_g = wl[:, 3 * A_W:o_ga].astype(BF16)
        w_ab = jnp.pad(wl[:, o_ga:o_gate], ((0, 0), (0, LANES - 2 * H_G))).astype(BF16)
        w_gate = wl[:, o_gate:].astype(BF16)
        wua, wug = w_up_a[l].astype(BF16), w_up_g[l].astype(BF16)
        wo, wpq = w_o[l].astype(BF16), w_pq[l].astype(BF16)
        keys = sub_keys[l].astype(BF16)
        pu, pv = peer_u[l].astype(BF16), peer_v[l].astype(BF16)
        lam_vecs = jnp.stack([lam_q1[l], lam_k1[l], lam_q2[l], lam_k2[l]])

        mod = ada_mod(jnp.concatenate([c_prompt, c_sample], axis=0), w_ada, b_ada[l], l)
        mod_p, mod_s = mod[:bp], mod[bp:]

        tm = 1024
        h = prenorm(yp, g_pre_mix[l], mod_p, False, seq, 256)
        q = matmul(h, w_q, BF16, tm, 512)
        k = matmul(h, w_k, F32, tm, 512)
        v = matmul(h, w_v, F32, tm, 512)
        gproj = matmul(h, w_g, F32, tm, 512)
        gab = matmul(h, w_ab, F32, tm, LANES)
        gates = matmul(h, w_gate, F32, tm, 512)
        o_a = attn_prompt(q, k, v, lam_vecs, g_attn_head[l], bp, seq, lam_init)
        o_g, s_p = gdn_prompt(gproj, gab, conv_w[l], a_log[l], dt_bias[l], g_gdn_head[l], bp, seq)
        merged = merge(o_a, o_g, wua, wug, gates, tm)
        mix = matmul(merged, wo, F32, tm, 512)
        x1, h2 = postmix(yp, mix, g_post_mix[l], g_pre_ffn[l], mod_p, False, seq, 256)
        f = peer(h2, wpq, keys, pu, pv, 512)
        yp = final_residual(x1, f, g_post_ffn[l], mod_p, False, seq, 256)
        outs[0].append(k.reshape(bp, seq, H_A, HD_A))
        outs[1].append(v.reshape(bp, seq, H_A, HD_A))
        outs[2].append(gproj.reshape(bp, seq, 4 * G_W)[:, seq - (CONV_W - 1):, :C_CONV])
        outs[3].append(s_p)

        hs = prenorm(ys, g_pre_mix[l], mod_s, True, 1, nb)
        qs = matmul(hs, w_q, F32, nb, 512)
        ks = matmul(hs, w_k, F32, nb, 512)
        vs = matmul(hs, w_v, F32, nb, 512)
        gproj_s = matmul(hs, w_g, F32, nb, 512)
        gab_s = matmul(hs, w_ab, F32, nb, LANES)
        gates_s = matmul(hs, w_gate, F32, nb, 512)
        oa_s = attn_decode(qs, ks, vs, cache_k, cache_v, page_table, lam_vecs,
                           g_attn_head[l], lam_init, l)
        og_s, s_s = gdn_decode(gproj_s, gab_s, state_conv[l], conv_w[l], a_log[l], dt_bias[l],
                               g_gdn_head[l], state_gdn[l])
        merged_s = merge(oa_s.astype(BF16), og_s.astype(BF16), wua, wug, gates_s, nb)
        mix_s = matmul(merged_s, wo, F32, nb, 512)
        x1s, h2s = postmix(ys, mix_s, g_post_mix[l], g_pre_ffn[l], mod_s, True, 1, nb)
        h2s_pad = jnp.pad(h2s, ((0, LANES - nb), (0, 0)))
        fs = peer(h2s_pad, wpq, keys, pu, pv, LANES)[:nb]
        ys = final_residual(x1s, fs, g_post_ffn[l], mod_s, True, 1, nb)
        outs[4].append(ks.reshape(nb, 1, H_A, HD_A))
        outs[5].append(vs.reshape(nb, 1, H_A, HD_A))
        outs[6].append(jnp.concatenate([state_conv[l][:, 1:], gproj_s[:, None, :C_CONV]], axis=1))
        outs[7].append(s_s)

    return (yp.reshape(bp, seq, D_MODEL), ys.reshape(nb, 1, D_MODEL),
            *[jnp.stack(o) for o in outs])
```

```python
import functools
import math

import jax
import jax.numpy as jnp
from jax import lax
from jax.experimental import pallas as pl
from jax.experimental.pallas import tpu as pltpu

F32 = jnp.float32
BF16 = jnp.bfloat16

D_MODEL = 4096
H_A = 8
DH_A = 128
HD_A = 2 * DH_A
A_W = H_A * HD_A
H_G = 16
DK_G = 128
DV_G = 128
G_W = H_G * DK_G
CONV_W = 4
C_CONV = 3 * G_W
GDN_CHUNK = 128
GDN_INTERLEAVE = 8
N_KEYS = 128
N_EXPERTS = N_KEYS * N_KEYS
P_HEADS = 8
P_TOPK = 16
PAGE = 128
EPS = 1e-6
LANES = 128
MIB = 2 ** 20

NT_DIMS = (((1,), (1,)), ((), ()))
TN_DIMS = (((0,), (0,)), ((), ()))


def _cp(semantics, vmem_mib=48):
    return pltpu.CompilerParams(dimension_semantics=semantics, vmem_limit_bytes=vmem_mib * MIB)


def _dot(a, b, dims=None, precision=None):
    if dims is None:
        return jnp.dot(a, b, preferred_element_type=F32, precision=precision)
    return lax.dot_general(a, b, dims, preferred_element_type=F32, precision=precision)


def _silu(x):
    return x * jax.nn.sigmoid(x)


def _rms(x, g):
    return x * lax.rsqrt(jnp.mean(x * x, axis=-1, keepdims=True) + EPS) * g


def _mod_body(c_ref, w_ref, b_ref, o_ref):
    sc = _silu(c_ref[...]).astype(BF16)
    o_ref[...] = _dot(sc, w_ref[...].astype(BF16)) + b_ref[...]


def ada_mod(c_all, w_ada, b_ada, layer):
    r, d = c_all.shape
    n = w_ada.shape[2]
    tn = 512
    return pl.pallas_call(
        _mod_body,
        grid=(n // tn,),
        in_specs=[pl.BlockSpec((r, d), lambda j: (0, 0)),
                  pl.BlockSpec((None, d, tn), lambda j: (layer, 0, j)),
                  pl.BlockSpec((1, tn), lambda j: (0, j))],
        out_specs=pl.BlockSpec((r, tn), lambda j: (0, j)),
        out_shape=jax.ShapeDtypeStruct((r, n), F32),
        compiler_params=_cp(("parallel",)),
        name="ada_mod",
    )(c_all, w_ada, b_ada.reshape(1, n))


def _mod_spec(per_token, tm, seq, chunk):
    if per_token:
        return pl.BlockSpec((tm, D_MODEL), lambda i: (i, chunk))
    return pl.BlockSpec((None, None, 1, D_MODEL), lambda i: ((i * tm) // seq, chunk, 0, 0))


def _mod_arg(mod, per_token):
    return mod if per_token else mod.reshape(mod.shape[0], 6, 1, D_MODEL)


def _prenorm_body(x_ref, g_ref, sc_ref, sh_ref, o_ref):
    y = _rms(x_ref[...], g_ref[...])
    o_ref[...] = (y * (1 + sc_ref[...]) + sh_ref[...]).astype(o_ref.dtype)


def prenorm(x, g, mod, per_token, seq, tm):
    m = x.shape[0]
    ma = _mod_arg(mod, per_token)
    return pl.pallas_call(
        _prenorm_body,
        grid=(m // tm,),
        in_specs=[pl.BlockSpec((tm, D_MODEL), lambda i: (i, 0)),
                  pl.BlockSpec((1, D_MODEL), lambda i: (0, 0)),
                  _mod_spec(per_token, tm, seq, 1),
                  _mod_spec(per_token, tm, seq, 0)],
        out_specs=pl.BlockSpec((tm, D_MODEL), lambda i: (i, 0)),
        out_shape=jax.ShapeDtypeStruct((m, D_MODEL), BF16),
        compiler_params=_cp(("parallel",)),
        name="prenorm",
    )(x, g.reshape(1, D_MODEL), ma, ma)


def _mm_body(x_ref, w_ref, o_ref):
    o_ref[...] = _dot(x_ref[...], w_ref[...]).astype(o_ref.dtype)


def matmul(x, w, out_dtype, tm, tn):
    m, k = x.shape
    n = w.shape[1]
    tm, tn = min(tm, m), min(tn, n)
    return pl.pallas_call(
        _mm_body,
        grid=(m // tm, n // tn),
        in_specs=[pl.BlockSpec((tm, k), lambda i, j: (i, 0)),
                  pl.BlockSpec((k, tn), lambda i, j: (0, j))],
        out_specs=pl.BlockSpec((tm, tn), lambda i, j: (i, j)),
        out_shape=jax.ShapeDtypeStruct((m, n), out_dtype),
        compiler_params=_cp(("parallel", "parallel")),
        name="matmul",
    )(x, w)


def _mm_nt_body(x_ref, wt_ref, o_ref):
    o_ref[...] = _dot(x_ref[...], wt_ref[...], NT_DIMS).astype(o_ref.dtype)


def matmul_nt(x, wt, out_dtype, tm, tn, row0=0, n=None):
    m, k = x.shape
    n = wt.shape[0] if n is None else n
    tm, tn = min(tm, m), min(tn, n)
    assert m % tm == 0 and n % tn == 0 and row0 % tn == 0
    j0 = row0 // tn
    return pl.pallas_call(
        _mm_nt_body,
        grid=(m // tm, n // tn),
        in_specs=[pl.BlockSpec((tm, k), lambda i, j: (i, 0)),
                  pl.BlockSpec((tn, k), lambda i, j: (j0 + j, 0))],
        out_specs=pl.BlockSpec((tm, tn), lambda i, j: (i, j)),
        out_shape=jax.ShapeDtypeStruct((m, n), out_dtype),
        compiler_params=_cp(("parallel", "parallel")),
        name="matmul_nt",
    )(x, wt)


W_IN_MAIN = 3 * A_W + 4 * G_W


ATTN_ROWS = 256


def _lam_value(lam_ref, lam_init):
    l = lam_ref[...]
    a = jnp.sum(l[0:1] * l[1:2], axis=-1, keepdims=True)
    b = jnp.sum(l[2:3] * l[3:4], axis=-1, keepdims=True)
    return jnp.exp(a) - jnp.exp(b) + lam_init


def _attn_body(qi_ref, ki_ref, lam_ref, q_ref, k_ref, v_ref, g_ref, o_ref, m_sc, l_sc, acc_sc, *,
               lam_init):
    qi = qi_ref[pl.program_id(2)]
    ki = ki_ref[pl.program_id(2)]
    tq, tk = q_ref.shape[0], k_ref.shape[0]
    scale = DH_A ** -0.5

    @pl.when(ki == 0)
    def _():
        m_sc[...] = jnp.full(m_sc.shape, -jnp.inf, F32)
        l_sc[...] = jnp.zeros(l_sc.shape, F32)
        acc_sc[...] = jnp.zeros(acc_sc.shape, F32)

    def update(diagonal):
        k = k_ref[...].astype(BF16)
        v = v_ref[...].astype(BF16)
        chains = [(g, c) for g in range(tq // ATTN_ROWS) for c in range(2)]
        rows = lambda g: slice(g * ATTN_ROWS, (g + 1) * ATTN_ROWS)
        lanes = lambda c: slice(c * DH_A, (c + 1) * DH_A)
        for g, c in chains:
            s = _dot(q_ref[rows(g), lanes(c)], k[:, lanes(c)], NT_DIMS) * scale
            if diagonal:
                row = lax.broadcasted_iota(jnp.int32, (ATTN_ROWS, tk), 0) + g * ATTN_ROWS
                col = lax.broadcasted_iota(jnp.int32, (ATTN_ROWS, tk), 1)
                s = jnp.where(col <= row, s, -jnp.inf)
            m_prev = m_sc[c, rows(g)]
            m_new = jnp.maximum(m_prev, jnp.max(s, axis=1, keepdims=True))
            alpha = jnp.exp(m_prev - m_new)
            m_sc[c, rows(g)] = m_new
            p = jnp.exp(s - m_new)
            l_sc[c, rows(g)] = alpha * l_sc[c, rows(g)] + jnp.sum(p, axis=1, keepdims=True)
            acc_sc[c, rows(g)] = alpha * acc_sc[c, rows(g)] + _dot(p.astype(BF16), v)

    @pl.when(ki < qi)
    def _():
        update(False)

    @pl.when(ki == qi)
    def _():
        update(True)

    @pl.when(ki == qi)
    def _():
        lam = _lam_value(lam_ref, lam_init)
        o = acc_sc[0] / l_sc[0] - lam * (acc_sc[1] / l_sc[1])
        o_ref[...] = (_rms(o, g_ref[...]) * (1 - lam_init)).astype(o_ref.dtype)


def attn_prompt(q, k, v, lam_vecs, g_head, batch, seq, lam_init, tq=512):
    nq = seq // tq
    pairs = [(i, j) for i in range(nq) for j in range(i + 1)]
    qi_tab = jnp.asarray([p[0] for p in pairs], jnp.int32)
    ki_tab = jnp.asarray([p[1] for p in pairs], jnp.int32)
    kernel = functools.partial(_attn_body, lam_init=lam_init)
    q_blk = pl.BlockSpec((tq, HD_A), lambda b, h, t, qt, kt: (b * nq + qt[t], h))
    kv_blk = pl.BlockSpec((tq, HD_A), lambda b, h, t, qt, kt: (b * nq + kt[t], h))
    return pl.pallas_call(
        kernel,
        grid_spec=pltpu.PrefetchScalarGridSpec(
            num_scalar_prefetch=2,
            grid=(batch, H_A, len(pairs)),
            in_specs=[pl.BlockSpec((4, DH_A), lambda b, h, t, qt, kt: (0, 0)),
                      q_blk, kv_blk, kv_blk,
                      pl.BlockSpec((1, HD_A), lambda b, h, t, qt, kt: (0, 0))],
            out_specs=q_blk,
            scratch_shapes=[pltpu.VMEM((2, tq, 1), F32), pltpu.VMEM((2, tq, 1), F32),
                            pltpu.VMEM((2, tq, HD_A), F32)]),
        out_shape=jax.ShapeDtypeStruct((batch * seq, A_W), BF16),
        compiler_params=_cp(("parallel", "parallel", "arbitrary")),
        name="attn_prompt",
    )(qi_tab, ki_tab, lam_vecs, q, k, v, g_head.reshape(1, HD_A))


DEC_PAGES = 8


def _head_rows(ref):
    rowi = lax.broadcasted_iota(jnp.int32, (2 * H_A, HD_A), 0)
    out = jnp.zeros((2 * H_A, HD_A), F32)
    for h in range(H_A):
        piece = jnp.broadcast_to(ref[:, h * HD_A:(h + 1) * HD_A], (2 * H_A, HD_A))
        out = jnp.where(rowi // 2 == h, piece, out)
    return out


def _attn_dec_body(pt_ref, lam_ref, q_ref, kn_ref, vn_ref, g_ref, *refs, lam_init):
    kc_refs, vc_refs = refs[:DEC_PAGES], refs[DEC_PAGES:2 * DEC_PAGES]
    o_ref, m_sc, l_sc, acc_sc = refs[2 * DEC_PAGES:]
    step = pl.program_id(1)
    scale = DH_A ** -0.5
    nr = 2 * H_A
    rowi = lax.broadcasted_iota(jnp.int32, (nr, HD_A), 0)
    comp = lax.broadcasted_iota(jnp.int32, (nr, HD_A), 1) // DH_A
    q16 = jnp.where(comp == rowi % 2, _head_rows(q_ref), 0.0).astype(BF16)
    srow = lax.broadcasted_iota(jnp.int32, (nr, PAGE * H_A), 0)
    scol = lax.broadcasted_iota(jnp.int32, (nr, PAGE * H_A), 1)
    own_head = scol % H_A == srow // 2

    @pl.when(step == 0)
    def _():
        m_sc[...] = jnp.full(m_sc.shape, -jnp.inf, F32)
        l_sc[...] = jnp.zeros(l_sc.shape, F32)
        acc_sc[...] = jnp.zeros(acc_sc.shape, F32)

    s = [jnp.where(own_head, _dot(q16, kc_refs[i][...].astype(BF16), NT_DIMS) * scale, -jnp.inf)
         for i in range(DEC_PAGES)]
    m_page = s[0].max(axis=1, keepdims=True)
    for i in range(1, DEC_PAGES):
        m_page = jnp.maximum(m_page, s[i].max(axis=1, keepdims=True))
    m_prev = m_sc[...]
    m_new = jnp.maximum(m_prev, m_page)
    alpha = jnp.exp(m_prev - m_new)
    l_new = alpha * l_sc[...]
    acc = alpha * acc_sc[...]
    for i in range(DEC_PAGES):
        pr = jnp.exp(s[i] - m_new)
        l_new = l_new + jnp.sum(pr, axis=1, keepdims=True)
        acc = acc + _dot(pr.astype(BF16), vc_refs[i][...].astype(BF16))
    l_sc[...] = l_new
    acc_sc[...] = acc
    m_sc[...] = m_new

    @pl.when(step == pl.num_programs(1) - 1)
    def _():
        kn = _head_rows(kn_ref).astype(BF16).astype(F32)
        vn = _head_rows(vn_ref).astype(BF16).astype(F32)
        sn = jnp.sum(q16.astype(F32) * kn, axis=1, keepdims=True) * scale
        m_fin = jnp.maximum(m_new, sn)
        beta = jnp.exp(m_new - m_fin)
        pn = jnp.exp(sn - m_fin)
        o2 = (beta * acc + pn.astype(BF16).astype(F32) * vn) / (beta * l_new + pn)
        lam = _lam_value(lam_ref, lam_init)
        for h in range(H_A):
            o = o2[2 * h:2 * h + 1] - lam * o2[2 * h + 1:2 * h + 2]
            o_ref[:, h * HD_A:(h + 1) * HD_A] = _rms(o, g_ref[...]) * (1 - lam_init)


def attn_decode(q, k_new, v_new, cache_k, cache_v, page_table, lam_vecs, g_head, lam_init, layer):
    nb, n_pages = page_table.shape
    assert n_pages % DEC_PAGES == 0
    depth, pool = cache_k.shape[:2]
    kc = cache_k.reshape(depth, pool, PAGE * H_A, HD_A)
    vc = cache_v.reshape(depth, pool, PAGE * H_A, HD_A)
    row3 = lambda a: a.reshape(nb, 1, A_W)
    kernel = functools.partial(_attn_dec_body, lam_init=lam_init)
    tok = pl.BlockSpec((None, 1, A_W), lambda b, p, pt: (b, 0, 0))

    def page(i):
        return pl.BlockSpec((None, None, PAGE * H_A, HD_A),
                            lambda b, p, pt: (layer, pt[b * n_pages + p * DEC_PAGES + i], 0, 0))

    pages = [page(i) for i in range(DEC_PAGES)]
    out = pl.pallas_call(
        kernel,
        grid_spec=pltpu.PrefetchScalarGridSpec(
            num_scalar_prefetch=1,
            grid=(nb, n_pages // DEC_PAGES),
            in_specs=[pl.BlockSpec((4, DH_A), lambda b, p, pt: (0, 0)), tok, tok, tok,
                      pl.BlockSpec((1, HD_A), lambda b, p, pt: (0, 0))] + pages + pages,
            out_specs=tok,
            scratch_shapes=[pltpu.VMEM((2 * H_A, 1), F32), pltpu.VMEM((2 * H_A, 1), F32),
                            pltpu.VMEM((2 * H_A, HD_A), F32)]),
        out_shape=jax.ShapeDtypeStruct((nb, 1, A_W), F32),
        compiler_params=_cp(("parallel", "arbitrary")),
        name="attn_decode",
    )(page_table.reshape(-1), lam_vecs, row3(q), row3(k_new), row3(v_new),
      g_head.reshape(1, HD_A), *([kc] * DEC_PAGES), *([vc] * DEC_PAGES))
    return out.reshape(nb, A_W)


def _softplus(x):
    return jnp.maximum(x, 0.0) + jnp.log1p(jnp.exp(-jnp.abs(x)))


def _gate_columns(gab, al_ref, dtb_ref, h):
    lane = lax.broadcasted_iota(jnp.int32, gab.shape, 1)
    g_all = -jnp.exp(al_ref[...]) * _softplus(gab + dtb_ref[...])
    g = jnp.sum(jnp.where(lane == h, g_all, 0.0), axis=1, keepdims=True)
    beta = jnp.sum(jnp.where(lane == H_G + h, jax.nn.sigmoid(gab), 0.0), axis=1, keepdims=True)
    return g, beta


def _gdn_body(xq_ref, xk_ref, xv_ref, z_ref, wq_ref, wk_ref, wv_ref, gab_ref, al_ref, dtb_ref, gh_ref,
              o_ref, s_ref, q_sc, k_sc, v_sc, gc_sc, beta_sc, qe_sc, oc_sc, m_sc, b_sc, gl_sc):
    h = pl.program_id(1)
    t = xq_ref.shape[0]
    c_len = GDN_CHUNK
    n_chunks = t // c_len
    row = lax.broadcasted_iota(jnp.int32, (t, DK_G), 0)

    def conv(x_ref, w_ref):
        x = x_ref[...]
        w = w_ref[...]
        y = jnp.where(row >= 3, pltpu.roll(x, 3, 0), 0.0) * w[0:1]
        y = y + jnp.where(row >= 2, pltpu.roll(x, 2, 0), 0.0) * w[1:2]
        y = y + jnp.where(row >= 1, pltpu.roll(x, 1, 0), 0.0) * w[2:3]
        y = y + x * w[3:4]
        return _silu(y)

    cq = conv(xq_ref, wq_ref)
    q_sc[...] = cq * lax.rsqrt(jnp.sum(cq * cq, axis=-1, keepdims=True) + EPS) * (DK_G ** -0.5)
    ck = conv(xk_ref, wk_ref)
    k_sc[...] = ck * lax.rsqrt(jnp.sum(ck * ck, axis=-1, keepdims=True) + EPS)
    v_sc[...] = conv(xv_ref, wv_ref)

    g, beta = _gate_columns(gab_ref[...], al_ref, dtb_ref, h)
    beta_sc[...] = jnp.broadcast_to(beta, (t, DK_G))
    gc = jnp.broadcast_to(g, (t, DK_G))
    rc = row % c_len
    shift = 1
    while shift < c_len:
        gc = gc + jnp.where(rc >= shift, pltpu.roll(gc, shift, 0), 0.0)
        shift *= 2
    gc_sc[...] = gc

    ii = lax.broadcasted_iota(jnp.int32, (c_len, c_len), 0)
    jj = lax.broadcasted_iota(jnp.int32, (c_len, c_len), 1)

    def decay_of(gcc):
        return jnp.exp(jnp.where(ii >= jj, gcc - gcc.T, -jnp.inf))

    def chunk_group(gi, carry):
        rows = [pl.ds(pl.multiple_of((gi * GDN_INTERLEAVE + u) * c_len, c_len), c_len)
                for u in range(GDN_INTERLEAVE)]
        low = []
        for r in rows:
            kc = k_sc[r, :]
            kk = _dot((kc * beta_sc[r, :]).astype(BF16), kc.astype(BF16), NT_DIMS)
            low.append(jnp.where(ii > jj, kk * decay_of(gc_sc[r, :]), 0.0))
        y = [-jnp.where(ii // 2 == jj // 2, l, 0.0) for l in low]
        s = 2
        while s < c_len:
            in_pair = (ii // (2 * s) == jj // (2 * s)) & (ii // s != jj // s)
            off = [jnp.where(in_pair, l, 0.0) for l in low]
            g = [o + _dot(t.astype(BF16), o.astype(BF16)) for t, o in zip(y, off)]
            y = [t - a - _dot(a.astype(BF16), t.astype(BF16)) for t, a in zip(y, g)]
            s *= 2
        for u, r in enumerate(rows):
            c = gi * GDN_INTERLEAVE + u
            qc, kc, gcc, bc = q_sc[r, :], k_sc[r, :], gc_sc[r, :], beta_sc[r, :]
            eg = jnp.exp(gcc)
            vb = v_sc[r, :] * bc
            kbg = kc * bc * eg
            yb = y[u].astype(BF16)
            ub = (vb + _dot(yb, vb.astype(BF16))).astype(BF16)
            wb = (kbg + _dot(yb, kbg.astype(BF16))).astype(BF16)
            qk = _dot(qc.astype(BF16), kc.astype(BF16), NT_DIMS)
            attn = jnp.where(ii >= jj, qk * decay_of(gcc), 0.0).astype(BF16)
            g_last = gcc[c_len - 1:c_len, :]
            kd = (kc * jnp.exp(g_last - gcc)).astype(BF16)
            qe_sc[r, :] = (qc * eg - _dot(attn, wb)).astype(BF16)
            oc_sc[r, :] = _dot(attn, ub)
            m_sc[c] = _dot(kd, wb, TN_DIMS).astype(BF16)
            b_sc[c] = _dot(kd, ub, TN_DIMS)
            gl_sc[c] = jnp.exp(g_last)
        return carry

    lax.fori_loop(0, n_chunks // GDN_INTERLEAVE, chunk_group, 0)

    def scan_emit(c, s):
        rows = pl.ds(pl.multiple_of(c * c_len, c_len), c_len)
        sb = s.astype(BF16)
        o = _dot(qe_sc[rows, :], sb) + oc_sc[rows, :]
        o_ref[rows, :] = (_rms(o, gh_ref[...]) * _silu(z_ref[rows, :])).astype(o_ref.dtype)
        return s * gl_sc[c] - _dot(m_sc[c], sb) + b_sc[c]

    s_ref[...] = lax.fori_loop(0, n_chunks, scan_emit, jnp.zeros((DK_G, DV_G), F32), unroll=2)


def gdn_prompt(gproj, gab, conv_w, a_log, dt_bias, g_head, batch, seq):
    hd = lambda off: pl.BlockSpec((seq, DK_G), lambda b, h: (b, off + h))
    cw = lambda off: pl.BlockSpec((CONV_W, DK_G), lambda b, h: (0, off + h))
    vec = pl.BlockSpec((1, LANES), lambda b, h: (0, 0))
    assert seq % (GDN_CHUNK * GDN_INTERLEAVE) == 0
    nc = seq // GDN_CHUNK
    pad = lambda a: jnp.pad(a, (0, LANES - a.shape[0])).reshape(1, LANES)
    o, s = pl.pallas_call(
        _gdn_body,
        grid=(batch, H_G),
        in_specs=[hd(0), hd(H_G), hd(2 * H_G), hd(3 * H_G), cw(0), cw(H_G), cw(2 * H_G),
                  pl.BlockSpec((seq, LANES), lambda b, h: (b, 0)), vec, vec, vec],
        out_specs=[pl.BlockSpec((seq, DV_G), lambda b, h: (b, h)),
                   pl.BlockSpec((None, None, DK_G, DV_G), lambda b, h: (b, h, 0, 0))],
        out_shape=[jax.ShapeDtypeStruct((batch * seq, G_W), BF16),
                   jax.ShapeDtypeStruct((batch, H_G, DK_G, DV_G), F32)],
        scratch_shapes=[pltpu.VMEM((seq, DK_G), F32), pltpu.VMEM((seq, DK_G), F32),
                        pltpu.VMEM((seq, DV_G), F32), pltpu.VMEM((seq, DK_G), F32),
                        pltpu.VMEM((seq, DK_G), F32), pltpu.VMEM((seq, DK_G), BF16),
                        pltpu.VMEM((seq, DV_G), F32), pltpu.VMEM((nc, DK_G, DK_G), BF16),
                        pltpu.VMEM((nc, DK_G, DV_G), F32), pltpu.VMEM((nc, 1, LANES), F32)],
        compiler_params=_cp(("parallel", "parallel")),
        name="gdn_prompt",
    )(gproj, gproj, gproj, gproj, conv_w, conv_w, conv_w, gab, pad(a_log), pad(dt_bias),
      g_head.reshape(1, DV_G))
    return o, s


def _gdn_dec_body(x_ref, prev_ref, w_ref, gab_ref, al_ref, dtb_ref, gh_ref, s0_ref, o_ref, s_ref):
    hi = lax.Precision.HIGHEST
    x = x_ref[...]
    prev = prev_ref[...]
    w = w_ref[...]
    y = prev[0:1] * w[0:1]
    y = y + prev[1:2] * w[1:2]
    y = y + prev[2:3] * w[2:3]
    y = _silu(y + x[:, :C_CONV] * w[3:4])
    gab = gab_ref[...]
    g_all = -jnp.exp(al_ref[...]) * _softplus(gab + dtb_ref[...])
    b_all = jax.nn.sigmoid(gab)
    r8 = lax.broadcasted_iota(jnp.int32, (8, DK_G), 0)
    for h in range(H_G):
        sl = lambda base: slice(base + h * DK_G, base + (h + 1) * DK_G)
        cq, ck, v = y[:, sl(0)], y[:, sl(G_W)], y[:, sl(2 * G_W)]
        qn = cq * lax.rsqrt(jnp.sum(cq * cq, axis=-1, keepdims=True) + EPS) * (DK_G ** -0.5)
        kn = ck * lax.rsqrt(jnp.sum(ck * ck, axis=-1, keepdims=True) + EPS)
        eg = jnp.exp(g_all[:, h:h + 1])
        beta = b_all[:, H_G + h:H_G + h + 1]
        s0 = s0_ref[h]
        lhs = jnp.where(r8 == 0, jnp.broadcast_to(kn, (8, DK_G)),
                        jnp.where(r8 == 1, jnp.broadcast_to(qn, (8, DK_G)), 0.0))
        proj = _dot(lhs, s0, precision=hi)
        v_corr = beta * v - (beta * eg) * proj[0:1]
        o = eg * proj[1:2] + jnp.sum(qn * kn, axis=-1, keepdims=True) * v_corr
        k8 = jnp.where(r8 == 0, jnp.broadcast_to(kn, (8, DK_G)), 0.0)
        v8 = jnp.broadcast_to(v_corr, (8, DV_G))
        s_ref[h] = s0 * eg + _dot(k8, v8, TN_DIMS, precision=hi)
        z = x[:, sl(3 * G_W)]
        o_ref[:, h * DV_G:(h + 1) * DV_G] = _rms(o, gh_ref[...]) * _silu(z)


def gdn_decode(gproj, gab, conv_prev, conv_w, a_log, dt_bias, g_head, s0):
    nb = gproj.shape[0]
    vec = pl.BlockSpec((1, LANES), lambda b: (0, 0))
    pad = lambda a: jnp.pad(a, (0, LANES - a.shape[0])).reshape(1, LANES)
    o, s = pl.pallas_call(
        _gdn_dec_body,
        grid=(nb,),
        in_specs=[pl.BlockSpec((None, 1, 4 * G_W), lambda b: (b, 0, 0)),
                  pl.BlockSpec((None, CONV_W - 1, C_CONV), lambda b: (b, 0, 0)),
                  pl.BlockSpec((CONV_W, C_CONV), lambda b: (0, 0)),
                  pl.BlockSpec((None, 1, LANES), lambda b: (b, 0, 0)), vec, vec, vec,
                  pl.BlockSpec((None, H_G, DK_G, DV_G), lambda b: (b, 0, 0, 0))],
        out_specs=[pl.BlockSpec((None, 1, G_W), lambda b: (b, 0, 0)),
                   pl.BlockSpec((None, H_G, DK_G, DV_G), lambda b: (b, 0, 0, 0))],
        out_shape=[jax.ShapeDtypeStruct((nb, 1, G_W), F32),
                   jax.ShapeDtypeStruct((nb, H_G, DK_G, DV_G), F32)],
        compiler_params=_cp(("parallel",)),
        name="gdn_decode",
    )(gproj.reshape(nb, 1, 4 * G_W), conv_prev, conv_w, gab.reshape(nb, 1, LANES),
      pad(a_log), pad(dt_bias), g_head.reshape(1, DV_G), s0)
    return o.reshape(nb, G_W), s


def _merge_body(oa_ref, og_ref, wa_ref, wg_ref, ga_ref, gg_ref, o_ref):
    a = jax.nn.sigmoid(ga_ref[...]) * _dot(oa_ref[...], wa_ref[...])
    g = jax.nn.sigmoid(gg_ref[...]) * _dot(og_ref[...], wg_ref[...])
    o_ref[...] = (a + g).astype(o_ref.dtype)


def merge(o_a, o_g, w_up_a, w_up_g, gates, tm, tn=512):
    m = o_a.shape[0]
    tm = min(tm, m)
    nj = D_MODEL // tn
    return pl.pallas_call(
        _merge_body,
        grid=(m // tm, nj),
        in_specs=[pl.BlockSpec((tm, A_W), lambda i, j: (i, 0)),
                  pl.BlockSpec((tm, G_W), lambda i, j: (i, 0)),
                  pl.BlockSpec((A_W, tn), lambda i, j: (0, j)),
                  pl.BlockSpec((G_W, tn), lambda i, j: (0, j)),
                  pl.BlockSpec((tm, tn), lambda i, j: (i, j)),
                  pl.BlockSpec((tm, tn), lambda i, j: (i, nj + j))],
        out_specs=pl.BlockSpec((tm, tn), lambda i, j: (i, j)),
        out_shape=jax.ShapeDtypeStruct((m, D_MODEL), BF16),
        compiler_params=_cp(("parallel", "parallel")),
        name="merge",
    )(o_a, o_g, w_up_a, w_up_g, gates, gates)


def _postmix_body(x_ref, mix_ref, gpost_ref, gpre_ref, ga_ref, sc_ref, sh_ref, x1_ref, h2_ref):
    x1 = x_ref[...] + ga_ref[...] * _rms(mix_ref[...], gpost_ref[...])
    x1_ref[...] = x1
    h2_ref[...] = (_rms(x1, gpre_ref[...]) * (1 + sc_ref[...]) + sh_ref[...]).astype(h2_ref.dtype)


def postmix(x, mix, g_post, g_pre, mod, per_token, seq, tm):
    m = x.shape[0]
    ma = _mod_arg(mod, per_token)
    rowblk = pl.BlockSpec((tm, D_MODEL), lambda i: (i, 0))
    vec = pl.BlockSpec((1, D_MODEL), lambda i: (0, 0))
    return pl.pallas_call(
        _postmix_body,
        grid=(m // tm,),
        in_specs=[rowblk, rowblk, vec, vec, _mod_spec(per_token, tm, seq, 2),
                  _mod_spec(per_token, tm, seq, 4), _mod_spec(per_token, tm, seq, 3)],
        out_specs=[rowblk, rowblk],
        out_shape=[jax.ShapeDtypeStruct((m, D_MODEL), F32), jax.ShapeDtypeStruct((m, D_MODEL), BF16)],
        compiler_params=_cp(("parallel",)),
        name="postmix",
    )(x, mix, g_post.reshape(1, D_MODEL), g_pre.reshape(1, D_MODEL), ma, ma, ma)


def _final_body(x_ref, f_ref, g_ref, ga_ref, o_ref):
    o_ref[...] = x_ref[...] + ga_ref[...] * _rms(f_ref[...], g_ref[...])


def final_residual(x1, f, g_post, mod, per_token, seq, tm):
    m = x1.shape[0]
    ma = _mod_arg(mod, per_token)
    rowblk = pl.BlockSpec((tm, D_MODEL), lambda i: (i, 0))
    return pl.pallas_call(
        _final_body,
        grid=(m // tm,),
        in_specs=[rowblk, rowblk, pl.BlockSpec((1, D_MODEL), lambda i: (0, 0)),
                  _mod_spec(per_token, tm, seq, 5)],
        out_specs=rowblk,
        out_shape=jax.ShapeDtypeStruct((m, D_MODEL), F32),
        compiler_params=_cp(("parallel",)),
        name="final_residual",
    )(x1, f, g_post.reshape(1, D_MODEL), ma)


def _top_values(x, k):
    vals = []
    for _ in range(k):
        m = jnp.max(x, axis=0, keepdims=True)
        vals.append(m)
        x = jnp.where(x == m, -jnp.inf, x)
    return jnp.concatenate(vals, axis=0)


def _route_body(q_ref, keys_ref, st_ref, stats_ref):
    tb = q_ref.shape[0]
    for h in range(P_HEADS):
        tops = []
        for p in range(2):
            col = (2 * h + p) * N_KEYS
            s = _dot(keys_ref[h, p], q_ref[:, col:col + N_KEYS], NT_DIMS)
            st_ref[h, p] = s
            tops.append(_top_values(s, P_TOPK))
        v1, v2 = tops
        cand = jnp.concatenate([v1[a:a + 1] + v2[0:P_TOPK // (a + 1)] for a in range(P_TOPK)], axis=0)
        best = _top_values(cand, P_TOPK)
        z = jnp.sum(jnp.exp(best - best[0:1]), axis=0, keepdims=True)
        stats_ref[h] = jnp.concatenate(
            [best[P_TOPK - 1:P_TOPK], v1[0:1], v2[0:1], 1.0 / z, jnp.zeros((4, tb), F32)], axis=0)


def peer_route(qp, keys, tb):
    n = qp.shape[0]
    return pl.pallas_call(
        _route_body,
        grid=(n // tb,),
        in_specs=[pl.BlockSpec((tb, 2 * P_HEADS * N_KEYS), lambda i: (i, 0)),
                  pl.BlockSpec((P_HEADS, 2, N_KEYS, N_KEYS), lambda i: (0, 0, 0, 0))],
        out_specs=[pl.BlockSpec((P_HEADS, 2, N_KEYS, tb), lambda i: (0, 0, 0, i)),
                   pl.BlockSpec((P_HEADS, 8, tb), lambda i: (0, 0, i))],
        out_shape=[jax.ShapeDtypeStruct((P_HEADS, 2, N_KEYS, n), F32),
                   jax.ShapeDtypeStruct((P_HEADS, 8, n), F32)],
        compiler_params=_cp(("parallel",)),
        name="peer_route",
    )(qp, keys)


def _peer_body(h_ref, u_ref, v_ref, st_ref, stats_ref, o_ref):
    ei = pl.program_id(1)
    eb = u_ref.shape[0]
    rows_per_step = eb // N_KEYS

    @pl.when(ei == 0)
    def _():
        o_ref[...] = jnp.zeros(o_ref.shape, F32)

    a = _dot(u_ref[...], h_ref[...], NT_DIMS)
    act = 0.5 * a * (1.0 + lax.erf(a * (2.0 ** -0.5)))
    parts = []
    for r in range(rows_per_step):
        i = ei * rows_per_step + r
        wsum = None
        for h in range(P_HEADS):
            s1 = st_ref[h, 0, pl.ds(i, 1), :]
            c1 = jnp.exp(s1 - stats_ref[h, 1:2, :]) * stats_ref[h, 3:4, :]
            s2 = st_ref[h, 1]
            wgt = jnp.where(s1 + s2 >= stats_ref[h, 0:1, :], jnp.exp(s2 - stats_ref[h, 2:3, :]) * c1, 0.0)
            wsum = wgt if wsum is None else wsum + wgt
        parts.append((wsum * act[r * N_KEYS:(r + 1) * N_KEYS]).astype(BF16))
    wg = jnp.concatenate(parts, axis=0)
    o_ref[...] += _dot(wg, v_ref[...], TN_DIMS)


def peer_experts(h2, u, v, st, stats, tb, eb=512):
    n = h2.shape[0]
    return pl.pallas_call(
        _peer_body,
        grid=(n // tb, N_EXPERTS // eb),
        in_specs=[pl.BlockSpec((tb, D_MODEL), lambda i, e: (i, 0)),
                  pl.BlockSpec((eb, D_MODEL), lambda i, e: (e, 0)),
                  pl.BlockSpec((eb, D_MODEL), lambda i, e: (e, 0)),
                  pl.BlockSpec((P_HEADS, 2, N_KEYS, tb), lambda i, e: (0, 0, 0, i)),
                  pl.BlockSpec((P_HEADS, 8, tb), lambda i, e: (0, 0, i))],
        out_specs=pl.BlockSpec((tb, D_MODEL), lambda i, e: (i, 0)),
        out_shape=jax.ShapeDtypeStruct((n, D_MODEL), F32),
        compiler_params=_cp(("parallel", "arbitrary"), vmem_mib=56),
        name="peer_experts",
    )(h2, u, v, st, stats)


def peer(h2, w_pq, keys, u, v, tb):
    qp = matmul(h2, w_pq, BF16, 1024, 512)
    st, stats = peer_route(qp, keys, min(tb, 256))
    return peer_experts(h2, u, v, st, stats, tb)


def kernel(x_prompt, x_sample, c_prompt, c_sample, cache_k, cache_v, page_table, state_conv, state_gdn, w_ada, b_ada, g_pre_mix, g_post_mix, g_pre_ffn, g_post_ffn, w_in, lam_q1, lam_k1, lam_q2, lam_k2, g_attn_head, conv_w, a_log, dt_bias, g_gdn_head, w_up_a, w_up_g, w_o, w_pq, sub_keys, peer_u, peer_v):
    depth = w_in.shape[0]
    bp, seq, _ = x_prompt.shape
    nb = x_sample.shape[0]
    yp = x_prompt.reshape(bp * seq, D_MODEL)
    ys = x_sample.reshape(nb, D_MODEL)
    outs = [[] for _ in range(8)]
    for l in range(depth):
        lam_init = 0.8 - 0.6 * math.exp(-0.3 * l)
        wt_main = jnp.swapaxes(w_in[l], 0, 1).astype(BF16)
        wt_gate = wt_main[W_IN_MAIN + 2 * H_G:]
        wt_ab = jnp.pad(wt_main[W_IN_MAIN:W_IN_MAIN + 2 * H_G],
                        ((0, LANES - 2 * H_G), (0, 0)))
        wua, wug = w_up_a[l].astype(BF16), w_up_g[l].astype(BF16)
        wo, wpq = w_o[l].astype(BF16), w_pq[l].astype(BF16)
        keys = sub_keys[l].astype(BF16)
        pu, pv = peer_u[l].astype(BF16), peer_v[l].astype(BF16)
        lam_vecs = jnp.stack([lam_q1[l], lam_k1[l], lam_q2[l], lam_k2[l]])

        mod = ada_mod(jnp.concatenate([c_prompt, c_sample], axis=0), w_ada, b_ada[l], l)
        mod_p, mod_s = mod[:bp], mod[bp:]

        tm = 1024
        h = prenorm(yp, g_pre_mix[l], mod_p, False, seq, 256)
        q = matmul_nt(h, wt_main, BF16, tm, 512, 0, A_W)
        k = matmul_nt(h, wt_main, F32, tm, 512, A_W, A_W)
        v = matmul_nt(h, wt_main, F32, tm, 512, 2 * A_W, A_W)
        gproj = matmul_nt(h, wt_main, F32, tm, 512, 3 * A_W, 4 * G_W)
        gates = matmul_nt(h, wt_gate, F32, tm, 512)
        gab = matmul_nt(h, wt_ab, F32, tm, LANES)
        o_a = attn_prompt(q, k, v, lam_vecs, g_attn_head[l], bp, seq, lam_init)
        o_g, s_p = gdn_prompt(gproj, gab, conv_w[l], a_log[l], dt_bias[l], g_gdn_head[l], bp, seq)
        merged = merge(o_a, o_g, wua, wug, gates, tm)
        mix = matmul(merged, wo, F32, tm, 512)
        x1, h2 = postmix(yp, mix, g_post_mix[l], g_pre_ffn[l], mod_p, False, seq, 256)
        f = peer(h2, wpq, keys, pu, pv, 512)
        yp = final_residual(x1, f, g_post_ffn[l], mod_p, False, seq, 256)
        outs[0].append(k.reshape(bp, seq, H_A, HD_A))
        outs[1].append(v.reshape(bp, seq, H_A, HD_A))
        outs[2].append(gproj.reshape(bp, seq, 4 * G_W)[:, seq - (CONV_W - 1):, :C_CONV])
        outs[3].append(s_p)

        hs = prenorm(ys, g_pre_mix[l], mod_s, True, 1, nb)
        proj_s = matmul_nt(hs, wt_main, F32, nb, 512, 0, W_IN_MAIN)
        qs, ks, vs = (proj_s[:, i * A_W:(i + 1) * A_W] for i in range(3))
        gproj_s = proj_s[:, 3 * A_W:]
        gates_s = matmul_nt(hs, wt_gate, F32, nb, 512)
        gab_s = matmul_nt(hs, wt_ab, F32, nb, LANES)
        oa_s = attn_decode(qs, ks, vs, cache_k, cache_v, page_table, lam_vecs,
                           g_attn_head[l], lam_init, l)
        og_s, s_s = gdn_decode(gproj_s, gab_s, state_conv[l], conv_w[l], a_log[l], dt_bias[l],
                               g_gdn_head[l], state_gdn[l])
        merged_s = merge(oa_s.astype(BF16), og_s.astype(BF16), wua, wug, gates_s, nb)
        mix_s = matmul(merged_s, wo, F32, nb, 512)
        x1s, h2s = postmix(ys, mix_s, g_post_mix[l], g_pre_ffn[l], mod_s, True, 1, nb)
        h2s_pad = jnp.pad(h2s, ((0, LANES - nb), (0, 0)))
        fs = peer(h2s_pad, wpq, keys, pu, pv, LANES)[:nb]
        ys = final_residual(x1s, fs, g_post_ffn[l], mod_s, True, 1, nb)
        outs[4].append(ks.reshape(nb, 1, H_A, HD_A))
        outs[5].append(vs.reshape(nb, 1, H_A, HD_A))
        outs[6].append(jnp.concatenate([state_conv[l][:, 1:], gproj_s[:, None, :C_CONV]], axis=1))
        outs[7].append(s_s)

    return (yp.reshape(bp, seq, D_MODEL), ys.reshape(nb, 1, D_MODEL),
            *[jnp.stack(o) for o in outs])
```

```python
import functools
import math

import jax
import jax.numpy as jnp
from jax import lax
from jax.experimental import pallas as pl
from jax.experimental.pallas import tpu as pltpu

F32 = jnp.float32
BF16 = jnp.bfloat16

D_MODEL = 4096
H_A = 8
DH_A = 128
HD_A = 2 * DH_A
A_W = H_A * HD_A
H_G = 16
DK_G = 128
DV_G = 128
G_W = H_G * DK_G
CONV_W = 4
C_CONV = 3 * G_W
GDN_CHUNK = 128
GDN_INTERLEAVE = 16
N_KEYS = 128
N_EXPERTS = N_KEYS * N_KEYS
P_HEADS = 8
P_TOPK = 16
PAGE = 128
EPS = 1e-6
LANES = 128
MIB = 2 ** 20

NT_DIMS = (((1,), (1,)), ((), ()))
TN_DIMS = (((0,), (0,)), ((), ()))


def _cp(semantics, vmem_mib=48):
    return pltpu.CompilerParams(dimension_semantics=semantics, vmem_limit_bytes=vmem_mib * MIB)


def _dot(a, b, dims=None, precision=None):
    if dims is None:
        return jnp.dot(a, b, preferred_element_type=F32, precision=precision)
    return lax.dot_general(a, b, dims, preferred_element_type=F32, precision=precision)


def _silu(x):
    return x * jax.nn.sigmoid(x)


def _rms(x, g):
    return x * lax.rsqrt(jnp.mean(x * x, axis=-1, keepdims=True) + EPS) * g


def _mod_body(c_ref, w_ref, b_ref, o_ref):
    sc = _silu(c_ref[...]).astype(BF16)
    o_ref[...] = _dot(sc, w_ref[...].astype(BF16)) + b_ref[...]


def ada_mod(c_all, w_ada, b_ada, layer):
    r, d = c_all.shape
    n = w_ada.shape[2]
    tn = 512
    return pl.pallas_call(
        _mod_body,
        grid=(n // tn,),
        in_specs=[pl.BlockSpec((r, d), lambda j: (0, 0)),
                  pl.BlockSpec((None, d, tn), lambda j: (layer, 0, j)),
                  pl.BlockSpec((1, tn), lambda j: (0, j))],
        out_specs=pl.BlockSpec((r, tn), lambda j: (0, j)),
        out_shape=jax.ShapeDtypeStruct((r, n), F32),
        compiler_params=_cp(("parallel",)),
        name="ada_mod",
    )(c_all, w_ada, b_ada.reshape(1, n))


def _mod_spec(per_token, tm, seq, chunk):
    if per_token:
        return pl.BlockSpec((tm, D_MODEL), lambda i: (i, chunk))
    return pl.BlockSpec((None, None, 1, D_MODEL), lambda i: ((i * tm) // seq, chunk, 0, 0))


def _mod_arg(mod, per_token):
    return mod if per_token else mod.reshape(mod.shape[0], 6, 1, D_MODEL)


def _prenorm_body(x_ref, g_ref, sc_ref, sh_ref, o_ref):
    y = _rms(x_ref[...], g_ref[...])
    o_ref[...] = (y * (1 + sc_ref[...]) + sh_ref[...]).astype(o_ref.dtype)


def prenorm(x, g, mod, per_token, seq, tm):
    m = x.shape[0]
    ma = _mod_arg(mod, per_token)
    return pl.pallas_call(
        _prenorm_body,
        grid=(m // tm,),
        in_specs=[pl.BlockSpec((tm, D_MODEL), lambda i: (i, 0)),
                  pl.BlockSpec((1, D_MODEL), lambda i: (0, 0)),
                  _mod_spec(per_token, tm, seq, 1),
                  _mod_spec(per_token, tm, seq, 0)],
        out_specs=pl.BlockSpec((tm, D_MODEL), lambda i: (i, 0)),
        out_shape=jax.ShapeDtypeStruct((m, D_MODEL), BF16),
        compiler_params=_cp(("parallel",)),
        name="prenorm",
    )(x, g.reshape(1, D_MODEL), ma, ma)


def _mm_body(x_ref, w_ref, o_ref):
    o_ref[...] = _dot(x_ref[...], w_ref[...]).astype(o_ref.dtype)


def matmul(x, w, out_dtype, tm, tn):
    m, k = x.shape
    n = w.shape[1]
    tm, tn = min(tm, m), min(tn, n)
    return pl.pallas_call(
        _mm_body,
        grid=(m // tm, n // tn),
        in_specs=[pl.BlockSpec((tm, k), lambda i, j: (i, 0)),
                  pl.BlockSpec((k, tn), lambda i, j: (0, j))],
        out_specs=pl.BlockSpec((tm, tn), lambda i, j: (i, j)),
        out_shape=jax.ShapeDtypeStruct((m, n), out_dtype),
        compiler_params=_cp(("parallel", "parallel")),
        name="matmul",
    )(x, w)


def _mm_nt_body(x_ref, wt_ref, o_ref):
    o_ref[...] = _dot(x_ref[...], wt_ref[...], NT_DIMS).astype(o_ref.dtype)


def matmul_nt(x, wt, out_dtype, tm, tn, row0=0, n=None):
    m, k = x.shape
    n = wt.shape[0] if n is None else n
    tm, tn = min(tm, m), min(tn, n)
    assert m % tm == 0 and n % tn == 0 and row0 % tn == 0
    j0 = row0 // tn
    return pl.pallas_call(
        _mm_nt_body,
        grid=(m // tm, n // tn),
        in_specs=[pl.BlockSpec((tm, k), lambda i, j: (i, 0)),
                  pl.BlockSpec((tn, k), lambda i, j: (j0 + j, 0))],
        out_specs=pl.BlockSpec((tm, tn), lambda i, j: (i, j)),
        out_shape=jax.ShapeDtypeStruct((m, n), out_dtype),
        compiler_params=_cp(("parallel", "parallel")),
        name="matmul_nt",
    )(x, wt)


W_IN_MAIN = 3 * A_W + 4 * G_W


ATTN_ROWS = 256


def _lam_value(lam_ref, lam_init):
    l = lam_ref[...]
    a = jnp.sum(l[0:1] * l[1:2], axis=-1, keepdims=True)
    b = jnp.sum(l[2:3] * l[3:4], axis=-1, keepdims=True)
    return jnp.exp(a) - jnp.exp(b) + lam_init


def _attn_body(qi_ref, ki_ref, lam_ref, q_ref, k_ref, v_ref, g_ref, o_ref, m_sc, l_sc, acc_sc, *,
               lam_init):
    qi = qi_ref[pl.program_id(2)]
    ki = ki_ref[pl.program_id(2)]
    tq, tk = q_ref.shape[0], k_ref.shape[0]
    scale = DH_A ** -0.5

    @pl.when(ki == 0)
    def _():
        m_sc[...] = jnp.full(m_sc.shape, -jnp.inf, F32)
        l_sc[...] = jnp.zeros(l_sc.shape, F32)
        acc_sc[...] = jnp.zeros(acc_sc.shape, F32)

    def update(diagonal):
        k = k_ref[...].astype(BF16)
        v = v_ref[...].astype(BF16)
        chains = [(g, c) for g in range(tq // ATTN_ROWS) for c in range(2)]
        rows = lambda g: slice(g * ATTN_ROWS, (g + 1) * ATTN_ROWS)
        lanes = lambda c: slice(c * DH_A, (c + 1) * DH_A)
        for g, c in chains:
            s = _dot(q_ref[rows(g), lanes(c)], k[:, lanes(c)], NT_DIMS) * scale
            if diagonal:
                row = lax.broadcasted_iota(jnp.int32, (ATTN_ROWS, tk), 0) + g * ATTN_ROWS
                col = lax.broadcasted_iota(jnp.int32, (ATTN_ROWS, tk), 1)
                s = jnp.where(col <= row, s, -jnp.inf)
            m_prev = m_sc[c, rows(g)]
            m_new = jnp.maximum(m_prev, jnp.max(s, axis=1, keepdims=True))
            alpha = jnp.exp(m_prev - m_new)
            m_sc[c, rows(g)] = m_new
            p = jnp.exp(s - m_new)
            l_sc[c, rows(g)] = alpha * l_sc[c, rows(g)] + jnp.sum(p, axis=1, keepdims=True)
            acc_sc[c, rows(g)] = alpha * acc_sc[c, rows(g)] + _dot(p.astype(BF16), v)

    @pl.when(ki < qi)
    def _():
        update(False)

    @pl.when(ki == qi)
    def _():
        update(True)

    @pl.when(ki == qi)
    def _():
        lam = _lam_value(lam_ref, lam_init)
        o = acc_sc[0] / l_sc[0] - lam * (acc_sc[1] / l_sc[1])
        o_ref[...] = (_rms(o, g_ref[...]) * (1 - lam_init)).astype(o_ref.dtype)


def attn_prompt(q, k, v, lam_vecs, g_head, batch, seq, lam_init, tq=512):
    nq = seq // tq
    pairs = [(i, j) for i in range(nq) for j in range(i + 1)]
    qi_tab = jnp.asarray([p[0] for p in pairs], jnp.int32)
    ki_tab = jnp.asarray([p[1] for p in pairs], jnp.int32)
    kernel = functools.partial(_attn_body, lam_init=lam_init)
    q_blk = pl.BlockSpec((tq, HD_A), lambda b, h, t, qt, kt: (b * nq + qt[t], h))
    kv_blk = pl.BlockSpec((tq, HD_A), lambda b, h, t, qt, kt: (b * nq + kt[t], h))
    return pl.pallas_call(
        kernel,
        grid_spec=pltpu.PrefetchScalarGridSpec(
            num_scalar_prefetch=2,
            grid=(batch, H_A, len(pairs)),
            in_specs=[pl.BlockSpec((4, DH_A), lambda b, h, t, qt, kt: (0, 0)),
                      q_blk, kv_blk, kv_blk,
                      pl.BlockSpec((1, HD_A), lambda b, h, t, qt, kt: (0, 0))],
            out_specs=q_blk,
            scratch_shapes=[pltpu.VMEM((2, tq, 1), F32), pltpu.VMEM((2, tq, 1), F32),
                            pltpu.VMEM((2, tq, HD_A), F32)]),
        out_shape=jax.ShapeDtypeStruct((batch * seq, A_W), BF16),
        compiler_params=_cp(("parallel", "parallel", "arbitrary")),
        name="attn_prompt",
    )(qi_tab, ki_tab, lam_vecs, q, k, v, g_head.reshape(1, HD_A))


DEC_PAGES = 8


def _head_rows(ref):
    rowi = lax.broadcasted_iota(jnp.int32, (2 * H_A, HD_A), 0)
    out = jnp.zeros((2 * H_A, HD_A), F32)
    for h in range(H_A):
        piece = jnp.broadcast_to(ref[:, h * HD_A:(h + 1) * HD_A], (2 * H_A, HD_A))
        out = jnp.where(rowi // 2 == h, piece, out)
    return out


def _attn_dec_body(pt_ref, lam_ref, q_ref, kn_ref, vn_ref, g_ref, *refs, lam_init):
    kc_refs, vc_refs = refs[:DEC_PAGES], refs[DEC_PAGES:2 * DEC_PAGES]
    o_ref, m_sc, l_sc, acc_sc = refs[2 * DEC_PAGES:]
    step = pl.program_id(1)
    scale = DH_A ** -0.5
    nr = 2 * H_A
    rowi = lax.broadcasted_iota(jnp.int32, (nr, HD_A), 0)
    comp = lax.broadcasted_iota(jnp.int32, (nr, HD_A), 1) // DH_A
    q16 = jnp.where(comp == rowi % 2, _head_rows(q_ref), 0.0).astype(BF16)
    srow = lax.broadcasted_iota(jnp.int32, (nr, PAGE * H_A), 0)
    scol = lax.broadcasted_iota(jnp.int32, (nr, PAGE * H_A), 1)
    own_head = scol % H_A == srow // 2

    @pl.when(step == 0)
    def _():
        m_sc[...] = jnp.full(m_sc.shape, -jnp.inf, F32)
        l_sc[...] = jnp.zeros(l_sc.shape, F32)
        acc_sc[...] = jnp.zeros(acc_sc.shape, F32)

    s = [jnp.where(own_head, _dot(q16, kc_refs[i][...].astype(BF16), NT_DIMS) * scale, -jnp.inf)
         for i in range(DEC_PAGES)]
    m_page = s[0].max(axis=1, keepdims=True)
    for i in range(1, DEC_PAGES):
        m_page = jnp.maximum(m_page, s[i].max(axis=1, keepdims=True))
    m_prev = m_sc[...]
    m_new = jnp.maximum(m_prev, m_page)
    alpha = jnp.exp(m_prev - m_new)
    l_new = alpha * l_sc[...]
    acc = alpha * acc_sc[...]
    for i in range(DEC_PAGES):
        pr = jnp.exp(s[i] - m_new)
        l_new = l_new + jnp.sum(pr, axis=1, keepdims=True)
        acc = acc + _dot(pr.astype(BF16), vc_refs[i][...].astype(BF16))
    l_sc[...] = l_new
    acc_sc[...] = acc
    m_sc[...] = m_new

    @pl.when(step == pl.num_programs(1) - 1)
    def _():
        kn = _head_rows(kn_ref).astype(BF16).astype(F32)
        vn = _head_rows(vn_ref).astype(BF16).astype(F32)
        sn = jnp.sum(q16.astype(F32) * kn, axis=1, keepdims=True) * scale
        m_fin = jnp.maximum(m_new, sn)
        beta = jnp.exp(m_new - m_fin)
        pn = jnp.exp(sn - m_fin)
        o2 = (beta * acc + pn.astype(BF16).astype(F32) * vn) / (beta * l_new + pn)
        lam = _lam_value(lam_ref, lam_init)
        for h in range(H_A):
            o = o2[2 * h:2 * h + 1] - lam * o2[2 * h + 1:2 * h + 2]
            o_ref[:, h * HD_A:(h + 1) * HD_A] = _rms(o, g_ref[...]) * (1 - lam_init)


def attn_decode(q, k_new, v_new, cache_k, cache_v, page_table, lam_vecs, g_head, lam_init, layer):
    nb, n_pages = page_table.shape
    assert n_pages % DEC_PAGES == 0
    depth, pool = cache_k.shape[:2]
    kc = cache_k.reshape(depth, pool, PAGE * H_A, HD_A)
    vc = cache_v.reshape(depth, pool, PAGE * H_A, HD_A)
    row3 = lambda a: a.reshape(nb, 1, A_W)
    kernel = functools.partial(_attn_dec_body, lam_init=lam_init)
    tok = pl.BlockSpec((None, 1, A_W), lambda b, p, pt: (b, 0, 0))

    def page(i):
        return pl.BlockSpec((None, None, PAGE * H_A, HD_A),
                            lambda b, p, pt: (layer, pt[b * n_pages + p * DEC_PAGES + i], 0, 0))

    pages = [page(i) for i in range(DEC_PAGES)]
    out = pl.pallas_call(
        kernel,
        grid_spec=pltpu.PrefetchScalarGridSpec(
            num_scalar_prefetch=1,
            grid=(nb, n_pages // DEC_PAGES),
            in_specs=[pl.BlockSpec((4, DH_A), lambda b, p, pt: (0, 0)), tok, tok, tok,
                      pl.BlockSpec((1, HD_A), lambda b, p, pt: (0, 0))] + pages + pages,
            out_specs=tok,
            scratch_shapes=[pltpu.VMEM((2 * H_A, 1), F32), pltpu.VMEM((2 * H_A, 1), F32),
                            pltpu.VMEM((2 * H_A, HD_A), F32)]),
        out_shape=jax.ShapeDtypeStruct((nb, 1, A_W), F32),
        compiler_params=_cp(("parallel", "arbitrary")),
        name="attn_decode",
    )(page_table.reshape(-1), lam_vecs, row3(q), row3(k_new), row3(v_new),
      g_head.reshape(1, HD_A), *([kc] * DEC_PAGES), *([vc] * DEC_PAGES))
    return out.reshape(nb, A_W)


def _softplus(x):
    return jnp.maximum(x, 0.0) + jnp.log1p(jnp.exp(-jnp.abs(x)))


def _gate_columns(gab, al_ref, dtb_ref, h):
    lane = lax.broadcasted_iota(jnp.int32, gab.shape, 1)
    g_all = -jnp.exp(al_ref[...]) * _softplus(gab + dtb_ref[...])
    g = jnp.sum(jnp.where(lane == h, g_all, 0.0), axis=1, keepdims=True)
    beta = jnp.sum(jnp.where(lane == H_G + h, jax.nn.sigmoid(gab), 0.0), axis=1, keepdims=True)
    return g, beta


def _gdn_body(xq_ref, xk_ref, xv_ref, z_ref, wq_ref, wk_ref, wv_ref, gab_ref, al_ref, dtb_ref, gh_ref,
              o_ref, s_ref, q_sc, k_sc, v_sc, gc_sc, beta_sc, qe_sc, oc_sc, m_sc, b_sc, gl_sc):
    h = pl.program_id(1)
    t = xq_ref.shape[0]
    c_len = GDN_CHUNK
    n_chunks = t // c_len
    row = lax.broadcasted_iota(jnp.int32, (t, DK_G), 0)

    def conv(x_ref, w_ref):
        x = x_ref[...]
        w = w_ref[...]
        y = jnp.where(row >= 3, pltpu.roll(x, 3, 0), 0.0) * w[0:1]
        y = y + jnp.where(row >= 2, pltpu.roll(x, 2, 0), 0.0) * w[1:2]
        y = y + jnp.where(row >= 1, pltpu.roll(x, 1, 0), 0.0) * w[2:3]
        y = y + x * w[3:4]
        return _silu(y)

    cq = conv(xq_ref, wq_ref)
    q_sc[...] = cq * lax.rsqrt(jnp.sum(cq * cq, axis=-1, keepdims=True) + EPS) * (DK_G ** -0.5)
    ck = conv(xk_ref, wk_ref)
    k_sc[...] = ck * lax.rsqrt(jnp.sum(ck * ck, axis=-1, keepdims=True) + EPS)
    v_sc[...] = conv(xv_ref, wv_ref)

    g, beta = _gate_columns(gab_ref[...], al_ref, dtb_ref, h)
    beta_sc[...] = jnp.broadcast_to(beta, (t, DK_G))
    gc = jnp.broadcast_to(g, (t, DK_G))
    rc = row % c_len
    shift = 1
    while shift < c_len:
        gc = gc + jnp.where(rc >= shift, pltpu.roll(gc, shift, 0), 0.0)
        shift *= 2
    gc_sc[...] = gc

    ii = lax.broadcasted_iota(jnp.int32, (c_len, c_len), 0)
    jj = lax.broadcasted_iota(jnp.int32, (c_len, c_len), 1)

    def decay_of(gcc):
        return jnp.exp(jnp.where(ii >= jj, gcc - gcc.T, -jnp.inf))

    def chunk_group(gi, carry):
        rows = [pl.ds(pl.multiple_of((gi * GDN_INTERLEAVE + u) * c_len, c_len), c_len)
                for u in range(GDN_INTERLEAVE)]
        low = []
        for r in rows:
            kc = k_sc[r, :]
            kk = _dot((kc * beta_sc[r, :]).astype(BF16), kc.astype(BF16), NT_DIMS)
            low.append(jnp.where(ii > jj, kk * decay_of(gc_sc[r, :]), 0.0))
        y = [-jnp.where(ii // 2 == jj // 2, l, 0.0) for l in low]
        s = 2
        while s < c_len:
            in_pair = (ii // (2 * s) == jj // (2 * s)) & (ii // s != jj // s)
            off = [jnp.where(in_pair, l, 0.0) for l in low]
            g = [o + _dot(t.astype(BF16), o.astype(BF16)) for t, o in zip(y, off)]
            y = [t - a - _dot(a.astype(BF16), t.astype(BF16)) for t, a in zip(y, g)]
            s *= 2
        for u, r in enumerate(rows):
            c = gi * GDN_INTERLEAVE + u
            qc, kc, gcc, bc = q_sc[r, :], k_sc[r, :], gc_sc[r, :], beta_sc[r, :]
            eg = jnp.exp(gcc)
            vb = v_sc[r, :] * bc
            kbg = kc * bc * eg
            rhs = jnp.concatenate([kbg, vb], axis=1)
            wu = (rhs + _dot(y[u].astype(BF16), rhs.astype(BF16))).astype(BF16)
            qk = _dot(qc.astype(BF16), kc.astype(BF16), NT_DIMS)
            attn = jnp.where(ii >= jj, qk * decay_of(gcc), 0.0).astype(BF16)
            g_last = gcc[c_len - 1:c_len, :]
            kd = (kc * jnp.exp(g_last - gcc)).astype(BF16)
            attn_wu = _dot(attn, wu)
            qe_sc[r, :] = (qc * eg - attn_wu[:, :DK_G]).astype(BF16)
            oc_sc[r, :] = attn_wu[:, DK_G:]
            kd_wu = _dot(kd, wu, TN_DIMS)
            m_sc[c] = kd_wu[:, :DK_G].astype(BF16)
            b_sc[c] = kd_wu[:, DK_G:]
            gl_sc[c] = jnp.exp(g_last)
        return carry

    lax.fori_loop(0, n_chunks // GDN_INTERLEAVE, chunk_group, 0)

    def scan_emit(c, s):
        rows = pl.ds(pl.multiple_of(c * c_len, c_len), c_len)
        sb = s.astype(BF16)
        o = _dot(qe_sc[rows, :], sb) + oc_sc[rows, :]
        o_ref[rows, :] = (_rms(o, gh_ref[...]) * _silu(z_ref[rows, :])).astype(o_ref.dtype)
        return s * gl_sc[c] - _dot(m_sc[c], sb) + b_sc[c]

    s_ref[...] = lax.fori_loop(0, n_chunks, scan_emit, jnp.zeros((DK_G, DV_G), F32), unroll=2)


def gdn_prompt(gproj, gab, conv_w, a_log, dt_bias, g_head, batch, seq):
    hd = lambda off: pl.BlockSpec((seq, DK_G), lambda b, h: (b, off + h))
    cw = lambda off: pl.BlockSpec((CONV_W, DK_G), lambda b, h: (0, off + h))
    vec = pl.BlockSpec((1, LANES), lambda b, h: (0, 0))
    assert seq % (GDN_CHUNK * GDN_INTERLEAVE) == 0
    nc = seq // GDN_CHUNK
    pad = lambda a: jnp.pad(a, (0, LANES - a.shape[0])).reshape(1, LANES)
    o, s = pl.pallas_call(
        _gdn_body,
        grid=(batch, H_G),
        in_specs=[hd(0), hd(H_G), hd(2 * H_G), hd(3 * H_G), cw(0), cw(H_G), cw(2 * H_G),
                  pl.BlockSpec((seq, LANES), lambda b, h: (b, 0)), vec, vec, vec],
        out_specs=[pl.BlockSpec((seq, DV_G), lambda b, h: (b, h)),
                   pl.BlockSpec((None, None, DK_G, DV_G), lambda b, h: (b, h, 0, 0))],
        out_shape=[jax.ShapeDtypeStruct((batch * seq, G_W), BF16),
                   jax.ShapeDtypeStruct((batch, H_G, DK_G, DV_G), F32)],
        scratch_shapes=[pltpu.VMEM((seq, DK_G), F32), pltpu.VMEM((seq, DK_G), F32),
                        pltpu.VMEM((seq, DV_G), F32), pltpu.VMEM((seq, DK_G), F32),
                        pltpu.VMEM((seq, DK_G), F32), pltpu.VMEM((seq, DK_G), BF16),
                        pltpu.VMEM((seq, DV_G), F32), pltpu.VMEM((nc, DK_G, DK_G), BF16),
                        pltpu.VMEM((nc, DK_G, DV_G), F32), pltpu.VMEM((nc, 1, LANES), F32)],
        compiler_params=_cp(("parallel", "parallel")),
        name="gdn_prompt",
    )(gproj, gproj, gproj, gproj, conv_w, conv_w, conv_w, gab, pad(a_log), pad(dt_bias),
      g_head.reshape(1, DV_G))
    return o, s


def _gdn_dec_body(x_ref, prev_ref, w_ref, gab_ref, al_ref, dtb_ref, gh_ref, s0_ref, o_ref, s_ref):
    x = x_ref[...]
    prev = prev_ref[...]
    w = w_ref[...]
    y = prev[0:1] * w[0:1]
    y = y + prev[1:2] * w[1:2]
    y = y + prev[2:3] * w[2:3]
    y = _silu(y + x[:, :C_CONV] * w[3:4])
    gab = gab_ref[...]
    g_all = -jnp.exp(al_ref[...]) * _softplus(gab + dtb_ref[...])
    b_all = jax.nn.sigmoid(gab)
    sl = lambda base, h: slice(base + h * DK_G, base + (h + 1) * DK_G)
    rows = lax.broadcasted_iota(jnp.int32, (LANES, DK_G), 0)
    kq = jnp.zeros((LANES, DK_G), F32)
    qk_dot = []
    for h in range(H_G):
        cq, ck = y[:, sl(0, h)], y[:, sl(G_W, h)]
        qn = cq * lax.rsqrt(jnp.sum(cq * cq, axis=-1, keepdims=True) + EPS) * (DK_G ** -0.5)
        kn = ck * lax.rsqrt(jnp.sum(ck * ck, axis=-1, keepdims=True) + EPS)
        qk_dot.append(jnp.sum(qn * kn, axis=-1, keepdims=True))
        kq = jnp.where(rows == 2 * h, jnp.broadcast_to(kn, (LANES, DK_G)), kq)
        kq = jnp.where(rows == 2 * h + 1, jnp.broadcast_to(qn, (LANES, DK_G)), kq)
    kq_t = kq.T
    for h in range(H_G):
        v = y[:, sl(2 * G_W, h)]
        eg = jnp.exp(g_all[:, h:h + 1])
        beta = b_all[:, H_G + h:H_G + h + 1]
        s0 = s0_ref[h]
        kcol = kq_t[:, 2 * h:2 * h + 1]
        qcol = kq_t[:, 2 * h + 1:2 * h + 2]
        ks = jnp.sum(s0 * kcol, axis=0, keepdims=True)
        qs = jnp.sum(s0 * qcol, axis=0, keepdims=True)
        v_corr = beta * v - (beta * eg) * ks
        o = eg * qs + qk_dot[h] * v_corr
        s_ref[h] = s0 * eg + kcol * v_corr
        z = x[:, sl(3 * G_W, h)]
        o_ref[:, h * DV_G:(h + 1) * DV_G] = _rms(o, gh_ref[...]) * _silu(z)


def gdn_decode(gproj, gab, conv_prev, conv_w, a_log, dt_bias, g_head, s0):
    nb = gproj.shape[0]
    vec = pl.BlockSpec((1, LANES), lambda b: (0, 0))
    pad = lambda a: jnp.pad(a, (0, LANES - a.shape[0])).reshape(1, LANES)
    o, s = pl.pallas_call(
        _gdn_dec_body,
        grid=(nb,),
        in_specs=[pl.BlockSpec((None, 1, 4 * G_W), lambda b: (b, 0, 0)),
                  pl.BlockSpec((None, CONV_W - 1, C_CONV), lambda b: (b, 0, 0)),
                  pl.BlockSpec((CONV_W, C_CONV), lambda b: (0, 0)),
                  pl.BlockSpec((None, 1, LANES), lambda b: (b, 0, 0)), vec, vec, vec,
                  pl.BlockSpec((None, H_G, DK_G, DV_G), lambda b: (b, 0, 0, 0))],
        out_specs=[pl.BlockSpec((None, 1, G_W), lambda b: (b, 0, 0)),
                   pl.BlockSpec((None, H_G, DK_G, DV_G), lambda b: (b, 0, 0, 0))],
        out_shape=[jax.ShapeDtypeStruct((nb, 1, G_W), F32),
                   jax.ShapeDtypeStruct((nb, H_G, DK_G, DV_G), F32)],
        compiler_params=_cp(("parallel",)),
        name="gdn_decode",
    )(gproj.reshape(nb, 1, 4 * G_W), conv_prev, conv_w, gab.reshape(nb, 1, LANES),
      pad(a_log), pad(dt_bias), g_head.reshape(1, DV_G), s0)
    return o.reshape(nb, G_W), s


def _merge_body(oa_ref, og_ref, wa_ref, wg_ref, ga_ref, gg_ref, o_ref):
    a = jax.nn.sigmoid(ga_ref[...]) * _dot(oa_ref[...], wa_ref[...])
    g = jax.nn.sigmoid(gg_ref[...]) * _dot(og_ref[...], wg_ref[...])
    o_ref[...] = (a + g).astype(o_ref.dtype)


def merge(o_a, o_g, w_up_a, w_up_g, gates, tm, tn=512):
    m = o_a.shape[0]
    tm = min(tm, m)
    nj = D_MODEL // tn
    return pl.pallas_call(
        _merge_body,
        grid=(m // tm, nj),
        in_specs=[pl.BlockSpec((tm, A_W), lambda i, j: (i, 0)),
                  pl.BlockSpec((tm, G_W), lambda i, j: (i, 0)),
                  pl.BlockSpec((A_W, tn), lambda i, j: (0, j)),
                  pl.BlockSpec((G_W, tn), lambda i, j: (0, j)),
                  pl.BlockSpec((tm, tn), lambda i, j: (i, j)),
                  pl.BlockSpec((tm, tn), lambda i, j: (i, nj + j))],
        out_specs=pl.BlockSpec((tm, tn), lambda i, j: (i, j)),
        out_shape=jax.ShapeDtypeStruct((m, D_MODEL), BF16),
        compiler_params=_cp(("parallel", "parallel")),
        name="merge",
    )(o_a, o_g, w_up_a, w_up_g, gates, gates)


def _postmix_body(x_ref, mix_ref, gpost_ref, gpre_ref, ga_ref, sc_ref, sh_ref, x1_ref, h2_ref):
    x1 = x_ref[...] + ga_ref[...] * _rms(mix_ref[...], gpost_ref[...])
    x1_ref[...] = x1
    h2_ref[...] = (_rms(x1, gpre_ref[...]) * (1 + sc_ref[...]) + sh_ref[...]).astype(h2_ref.dtype)


def postmix(x, mix, g_post, g_pre, mod, per_token, seq, tm):
    m = x.shape[0]
    ma = _mod_arg(mod, per_token)
    rowblk = pl.BlockSpec((tm, D_MODEL), lambda i: (i, 0))
    vec = pl.BlockSpec((1, D_MODEL), lambda i: (0, 0))
    return pl.pallas_call(
        _postmix_body,
        grid=(m // tm,),
        in_specs=[rowblk, rowblk, vec, vec, _mod_spec(per_token, tm, seq, 2),
                  _mod_spec(per_token, tm, seq, 4), _mod_spec(per_token, tm, seq, 3)],
        out_specs=[rowblk, rowblk],
        out_shape=[jax.ShapeDtypeStruct((m, D_MODEL), F32), jax.ShapeDtypeStruct((m, D_MODEL), BF16)],
        compiler_params=_cp(("parallel",)),
        name="postmix",
    )(x, mix, g_post.reshape(1, D_MODEL), g_pre.reshape(1, D_MODEL), ma, ma, ma)


def _final_body(x_ref, f_ref, g_ref, ga_ref, o_ref):
    o_ref[...] = x_ref[...] + ga_ref[...] * _rms(f_ref[...], g_ref[...])


def final_residual(x1, f, g_post, mod, per_token, seq, tm):
    m = x1.shape[0]
    ma = _mod_arg(mod, per_token)
    rowblk = pl.BlockSpec((tm, D_MODEL), lambda i: (i, 0))
    return pl.pallas_call(
        _final_body,
        grid=(m // tm,),
        in_specs=[rowblk, rowblk, pl.BlockSpec((1, D_MODEL), lambda i: (0, 0)),
                  _mod_spec(per_token, tm, seq, 5)],
        out_specs=rowblk,
        out_shape=jax.ShapeDtypeStruct((m, D_MODEL), F32),
        compiler_params=_cp(("parallel",)),
        name="final_residual",
    )(x1, f, g_post.reshape(1, D_MODEL), ma)


def _top_values(x, k):
    vals = []
    for _ in range(k):
        m = jnp.max(x, axis=0, keepdims=True)
        vals.append(m)
        x = jnp.where(x == m, -jnp.inf, x)
    return jnp.concatenate(vals, axis=0)


def _top_values_of_keys(x, k):
    n = x.shape[0] // 8
    assert n == 16 and k <= n
    v = [x[8 * i:8 * (i + 1)] for i in range(n)]
    size = 2
    while size <= n:
        j = size // 2
        while j >= 1:
            for i in range(n):
                l = i ^ j
                if l > i:
                    lo, hi = jnp.minimum(v[i], v[l]), jnp.maximum(v[i], v[l])
                    v[i], v[l] = (hi, lo) if (i & size) == 0 else (lo, hi)
            j //= 2
        size *= 2
    vals = []
    for t in range(k):
        m = jnp.max(v[0], axis=0, keepdims=True)
        vals.append(m)
        pop = v[0] == m
        for i in range(k - 1 - t):
            v[i] = jnp.where(pop, v[i + 1], v[i])
    return jnp.concatenate(vals, axis=0)


def _route_body(q_ref, keys_ref, st_ref, stats_ref):
    tb = q_ref.shape[0]
    for h in range(P_HEADS):
        tops = []
        for p in range(2):
            col = (2 * h + p) * N_KEYS
            s = _dot(keys_ref[h, p], q_ref[:, col:col + N_KEYS], NT_DIMS)
            st_ref[h, p] = s
            tops.append(_top_values_of_keys(s, P_TOPK))
        v1, v2 = tops
        cand = jnp.concatenate([v1[a:a + 1] + v2[0:P_TOPK // (a + 1)] for a in range(P_TOPK)], axis=0)
        best = _top_values(cand, P_TOPK)
        z = jnp.sum(jnp.exp(best - best[0:1]), axis=0, keepdims=True)
        stats_ref[h] = jnp.concatenate(
            [best[P_TOPK - 1:P_TOPK], v1[0:1], v2[0:1], 1.0 / z, jnp.zeros((4, tb), F32)], axis=0)


def peer_route(qp, keys, tb):
    n = qp.shape[0]
    return pl.pallas_call(
        _route_body,
        grid=(n // tb,),
        in_specs=[pl.BlockSpec((tb, 2 * P_HEADS * N_KEYS), lambda i: (i, 0)),
                  pl.BlockSpec((P_HEADS, 2, N_KEYS, N_KEYS), lambda i: (0, 0, 0, 0))],
        out_specs=[pl.BlockSpec((P_HEADS, 2, N_KEYS, tb), lambda i: (0, 0, 0, i)),
                   pl.BlockSpec((P_HEADS, 8, tb), lambda i: (0, 0, i))],
        out_shape=[jax.ShapeDtypeStruct((P_HEADS, 2, N_KEYS, n), F32),
                   jax.ShapeDtypeStruct((P_HEADS, 8, n), F32)],
        compiler_params=_cp(("parallel",)),
        name="peer_route",
    )(qp, keys)


def _peer_body(h_ref, u_ref, v_ref, st_ref, stats_ref, o_ref):
    ei = pl.program_id(1)
    eb = u_ref.shape[0]
    rows_per_step = eb // N_KEYS

    @pl.when(ei == 0)
    def _():
        o_ref[...] = jnp.zeros(o_ref.shape, F32)

    a = _dot(u_ref[...], h_ref[...], NT_DIMS)
    act = 0.5 * a * (1.0 + lax.erf(a * (2.0 ** -0.5)))
    parts = []
    for r in range(rows_per_step):
        i = ei * rows_per_step + r
        wsum = None
        for h in range(P_HEADS):
            s1 = st_ref[h, 0, pl.ds(i, 1), :]
            c1 = jnp.exp(s1 - stats_ref[h, 1:2, :]) * stats_ref[h, 3:4, :]
            s2 = st_ref[h, 1]
            wgt = jnp.where(s1 + s2 >= stats_ref[h, 0:1, :], jnp.exp(s2 - stats_ref[h, 2:3, :]) * c1, 0.0)
            wsum = wgt if wsum is None else wsum + wgt
        parts.append((wsum * act[r * N_KEYS:(r + 1) * N_KEYS]).astype(BF16))
    wg = jnp.concatenate(parts, axis=0)
    o_ref[...] += _dot(wg, v_ref[...], TN_DIMS)


def peer_experts(h2, u, v, st, stats, tb, eb=512):
    n = h2.shape[0]
    return pl.pallas_call(
        _peer_body,
        grid=(n // tb, N_EXPERTS // eb),
        in_specs=[pl.BlockSpec((tb, D_MODEL), lambda i, e: (i, 0)),
                  pl.BlockSpec((eb, D_MODEL), lambda i, e: (e, 0)),
                  pl.BlockSpec((eb, D_MODEL), lambda i, e: (e, 0)),
                  pl.BlockSpec((P_HEADS, 2, N_KEYS, tb), lambda i, e: (0, 0, 0, i)),
                  pl.BlockSpec((P_HEADS, 8, tb), lambda i, e: (0, 0, i))],
        out_specs=pl.BlockSpec((tb, D_MODEL), lambda i, e: (i, 0)),
        out_shape=jax.ShapeDtypeStruct((n, D_MODEL), F32),
        compiler_params=_cp(("parallel", "arbitrary"), vmem_mib=56),
        name="peer_experts",
    )(h2, u, v, st, stats)


def peer(h2, w_pq, keys, u, v, tb):
    qp = matmul(h2, w_pq, BF16, 1024, 512)
    st, stats = peer_route(qp, keys, min(tb, 256))
    return peer_experts(h2, u, v, st, stats, tb)


def kernel(x_prompt, x_sample, c_prompt, c_sample, cache_k, cache_v, page_table, state_conv, state_gdn, w_ada, b_ada, g_pre_mix, g_post_mix, g_pre_ffn, g_post_ffn, w_in, lam_q1, lam_k1, lam_q2, lam_k2, g_attn_head, conv_w, a_log, dt_bias, g_gdn_head, w_up_a, w_up_g, w_o, w_pq, sub_keys, peer_u, peer_v):
    depth = w_in.shape[0]
    bp, seq, _ = x_prompt.shape
    nb = x_sample.shape[0]
    yp = x_prompt.reshape(bp * seq, D_MODEL)
    ys = x_sample.reshape(nb, D_MODEL)
    outs = [[] for _ in range(8)]
    for l in range(depth):
        lam_init = 0.8 - 0.6 * math.exp(-0.3 * l)
        wt_main = jnp.swapaxes(w_in[l], 0, 1).astype(BF16)
        wt_gate = wt_main[W_IN_MAIN + 2 * H_G:]
        wt_ab = jnp.pad(wt_main[W_IN_MAIN:W_IN_MAIN + 2 * H_G],
                        ((0, LANES - 2 * H_G), (0, 0)))
        wua, wug = w_up_a[l].astype(BF16), w_up_g[l].astype(BF16)
        wo, wpq = w_o[l].astype(BF16), w_pq[l].astype(BF16)
        keys = sub_keys[l].astype(BF16)
        pu, pv = peer_u[l].astype(BF16), peer_v[l].astype(BF16)
        lam_vecs = jnp.stack([lam_q1[l], lam_k1[l], lam_q2[l], lam_k2[l]])

        mod = ada_mod(jnp.concatenate([c_prompt, c_sample], axis=0), w_ada, b_ada[l], l)
        mod_p, mod_s = mod[:bp], mod[bp:]

        tm = 1024
        h = prenorm(yp, g_pre_mix[l], mod_p, False, seq, 256)
        q = matmul_nt(h, wt_main, BF16, tm, 512, 0, A_W)
        k = matmul_nt(h, wt_main, F32, tm, 512, A_W, A_W)
        v = matmul_nt(h, wt_main, F32, tm, 512, 2 * A_W, A_W)
        gproj = matmul_nt(h, wt_main, F32, tm, 512, 3 * A_W, 4 * G_W)
        gates = matmul_nt(h, wt_gate, F32, tm, 512)
        gab = matmul_nt(h, wt_ab, F32, tm, LANES)
        o_a = attn_prompt(q, k, v, lam_vecs, g_attn_head[l], bp, seq, lam_init)
        o_g, s_p = gdn_prompt(gproj, gab, conv_w[l], a_log[l], dt_bias[l], g_gdn_head[l], bp, seq)
        merged = merge(o_a, o_g, wua, wug, gates, tm)
        mix = matmul(merged, wo, F32, tm, 512)
        x1, h2 = postmix(yp, mix, g_post_mix[l], g_pre_ffn[l], mod_p, False, seq, 256)
        f = peer(h2, wpq, keys, pu, pv, 512)
        yp = final_residual(x1, f, g_post_ffn[l], mod_p, False, seq, 256)
        outs[0].append(k.reshape(bp, seq, H_A, HD_A))
        outs[1].append(v.reshape(bp, seq, H_A, HD_A))
        outs[2].append(gproj.reshape(bp, seq, 4 * G_W)[:, seq - (CONV_W - 1):, :C_CONV])
        outs[3].append(s_p)

        hs = prenorm(ys, g_pre_mix[l], mod_s, True, 1, nb)
        proj_s = matmul_nt(hs, wt_main, F32, nb, 512, 0, W_IN_MAIN)
        qs, ks, vs = (proj_s[:, i * A_W:(i + 1) * A_W] for i in range(3))
        gproj_s = proj_s[:, 3 * A_W:]
        gates_s = matmul_nt(hs, wt_gate, F32, nb, 512)
        gab_s = matmul_nt(hs, wt_ab, F32, nb, LANES)
        oa_s = attn_decode(qs, ks, vs, cache_k, cache_v, page_table, lam_vecs,
                           g_attn_head[l], lam_init, l)
        og_s, s_s = gdn_decode(gproj_s, gab_s, state_conv[l], conv_w[l], a_log[l], dt_bias[l],
                               g_gdn_head[l], state_gdn[l])
        merged_s = merge(oa_s.astype(BF16), og_s.astype(BF16), wua, wug, gates_s, nb)
        mix_s = matmul(merged_s, wo, F32, nb, 512)
        x1s, h2s = postmix(ys, mix_s, g_post_mix[l], g_pre_ffn[l], mod_s, True, 1, nb)
        h2s_pad = jnp.pad(h2s, ((0, LANES - nb), (0, 0)))
        fs = peer(h2s_pad, wpq, keys, pu, pv, LANES)[:nb]
        ys = final_residual(x1s, fs, g_post_ffn[l], mod_s, True, 1, nb)
        outs[4].append(ks.reshape(nb, 1, H_A, HD_A))
        outs[5].append(vs.reshape(nb, 1, H_A, HD_A))
        outs[6].append(jnp.concatenate([state_conv[l][:, 1:], gproj_s[:, None, :C_CONV]], axis=1))
        outs[7].append(s_s)

    return (yp.reshape(bp, seq, D_MODEL), ys.reshape(nb, 1, D_MODEL),
            *[jnp.stack(o) for o in outs])
```
